```python
import math
import jax, jax.numpy as jnp
from jax import lax
import numpy as np

D_MODEL = 1024
BATCH = 2
SEQ = 8192
DEPTH = 1

D_MIX = D_MODEL
G_WIDTH = D_MIX // 2
R_WIDTH = D_MIX - G_WIDTH
G_HEADS = 4
G_HEAD_DIM = G_WIDTH // G_HEADS
CHUNK = 128
R_HEADS = 8
R_HEAD_DIM = R_WIDTH // R_HEADS
CONV_W = 4
RG_C = 8.0
N_MEM = 256
X_HEADS = 4
X_HEAD_DIM = D_MODEL // X_HEADS
D_FF = int(math.ceil(D_MODEL * 8 / 3 / 256) * 256)
IN_COLS = 2 * G_WIDTH + 2 * R_WIDTH
EPS = 1e-6

kernel_name = "hybrid_gmlp_rglru_xattn_block"


def rms_norm(x, g):
    xf = x.astype(jnp.float32)
    y = xf * lax.rsqrt(jnp.mean(xf * xf, axis=-1, keepdims=True) + EPS)
    return (y * g.astype(jnp.float32)).astype(x.dtype)


def layer_norm(x, g, b):
    xf = x.astype(jnp.float32)
    mu = jnp.mean(xf, axis=-1, keepdims=True)
    var = jnp.mean(jnp.square(xf - mu), axis=-1, keepdims=True)
    y = (xf - mu) * lax.rsqrt(var + EPS)
    return (y * g.astype(jnp.float32) + b.astype(jnp.float32)).astype(x.dtype)


def gmlp_group(zu, zv, ln_v_g, ln_v_b, w_s, b_s):
    B, S, _ = zu.shape
    u = jax.nn.gelu(zu)
    v = layer_norm(jax.nn.gelu(zv), ln_v_g, ln_v_b)
    vb = v.reshape(B, S // CHUNK, CHUNK, G_HEADS, G_HEAD_DIM)
    causal = jnp.tril(jnp.ones((CHUNK, CHUNK), dtype=bool))
    ws = jnp.where(causal[None], w_s, jnp.zeros_like(w_s))
    s = jnp.einsum('htp,bnphd->bnthd', ws, vb) + b_s.T[None, None, :, :, None]
    return u * s.reshape(B, S, G_WIDTH)


def causal_dwconv(x, w, b):
    S = x.shape[1]
    xp = jnp.pad(x, ((0, 0), (CONV_W - 1, 0), (0, 0)))
    y = b
    for k in range(CONV_W):
        y = y + xp[:, k:k + S, :] * w[k]
    return y


def rglru_group(xr, gr, conv_w, conv_b, w_a, b_a, w_x, b_x, lam):
    B, S, _ = xr.shape
    xc = causal_dwconv(xr, conv_w, conv_b)
    xh = xc.reshape(B, S, R_HEADS, R_HEAD_DIM)
    r = jax.nn.sigmoid(jnp.einsum('bshi,hij->bshj', xh, w_a).reshape(B, S, R_WIDTH) + b_a)
    i = jax.nn.sigmoid(jnp.einsum('bshi,hij->bshj', xh, w_x).reshape(B, S, R_WIDTH) + b_x)
    log_a = -RG_C * r.astype(jnp.float32) * jax.nn.softplus(-lam.astype(jnp.float32))
    a = jnp.exp(log_a)
    mult = jnp.sqrt(-jnp.expm1(2.0 * log_a))
    bterm = mult * (i * xc).astype(jnp.float32)

    def combine(left, right):
        a_l, b_l = left
        a_r, b_r = right
        return a_l * a_r, a_r * b_l + b_r

    _, h = lax.associative_scan(combine, (a, bterm), axis=1)
    return jax.nn.gelu(gr) * h.astype(xr.dtype)


def token_mixer(h, w_in, ln_v_g, ln_v_b, w_s, b_s, conv_w, conv_b, w_a, b_a,
                w_x, b_x, lam, g_out_gmlp, g_out_lru, w_out):
    z = h @ w_in
    zu, zv, xr, gr = jnp.split(z, [G_WIDTH, 2 * G_WIDTH, 2 * G_WIDTH + R_WIDTH], axis=-1)
    y_g = gmlp_group(zu, zv, ln_v_g, ln_v_b, w_s, b_s)
    y_r = rglru_group(xr, gr, conv_w, conv_b, w_a, b_a, w_x, b_x, lam)
    y = jnp.concatenate([rms_norm(y_g, g_out_gmlp), rms_norm(y_r, g_out_lru)], axis=-1)
    return y @ w_out


def mem_cross_attention(h, m, w_q, w_kv, w_o):
    B, S, _ = h.shape
    q = (h @ w_q).reshape(B, S, X_HEADS, X_HEAD_DIM)
    k, v = jnp.split(m @ w_kv, 2, axis=-1)
    k = k.reshape(B, N_MEM, X_HEADS, X_HEAD_DIM)
    v = v.reshape(B, N_MEM, X_HEADS, X_HEAD_DIM)
    scores = jnp.einsum('bshd,bmhd->bhsm', q, k).astype(jnp.float32) * (X_HEAD_DIM ** -0.5)
    p = jax.nn.softmax(scores, axis=-1).astype(v.dtype)
    o = jnp.einsum('bhsm,bmhd->bshd', p, v).reshape(B, S, X_HEADS * X_HEAD_DIM)
    return o @ w_o


def swiglu(h, w_gate, w_up, w_down):
    return (jax.nn.silu(h @ w_gate) * (h @ w_up)) @ w_down


def setup_inputs(seed: int = 0) -> dict:
    key = jax.random.key(seed)
    ks = iter(jax.random.split(key, 40))
    f32 = jnp.float32

    def nrm(shape, scale):
        return jax.random.normal(next(ks), shape, f32) * scale

    def gain(shape):
        return 1.0 + 0.02 * jax.random.normal(next(ks), shape, f32)

    Lr = DEPTH
    u = jax.random.uniform(next(ks), (Lr, R_WIDTH), f32, 0.9, 0.999)
    a0 = u ** (1.0 / RG_C)
    lam = jnp.log(a0) - jnp.log1p(-a0)
    return {
        "x": jax.random.normal(next(ks), (BATCH, SEQ, D_MODEL), f32),
        "mem": jax.random.normal(next(ks), (BATCH, N_MEM, D_MODEL), f32),
        "w_in": nrm((Lr, D_MODEL, IN_COLS), D_MODEL ** -0.5),
        "ln_v_g": gain((Lr, G_WIDTH)),
        "ln_v_b": nrm((Lr, G_WIDTH), 0.02),
        "w_s": nrm((Lr, G_HEADS, CHUNK, CHUNK), CHUNK ** -0.5),
        "b_s": gain((Lr, G_HEADS, CHUNK)),
        "conv_w": nrm((Lr, CONV_W, R_WIDTH), CONV_W ** -0.5),
        "conv_b": nrm((Lr, R_WIDTH), 0.02),
        "w_a": nrm((Lr, R_HEADS, R_HEAD_DIM, R_HEAD_DIM), R_HEAD_DIM ** -0.5),
        "b_a": nrm((Lr, R_WIDTH), 0.02),
        "w_x": nrm((Lr, R_HEADS, R_HEAD_DIM, R_HEAD_DIM), R_HEAD_DIM ** -0.5),
        "b_x": nrm((Lr, R_WIDTH), 0.02),
        "lam": lam,
        "g_out_gmlp": gain((Lr, G_WIDTH)),
        "g_out_lru": gain((Lr, R_WIDTH)),
        "w_out": nrm((Lr, D_MIX, D_MODEL), D_MIX ** -0.5),
        "w_q": nrm((Lr, D_MODEL, X_HEADS * X_HEAD_DIM), D_MODEL ** -0.5),
        "w_kv": nrm((Lr, D_MODEL, 2 * X_HEADS * X_HEAD_DIM), D_MODEL ** -0.5),
        "w_o": nrm((Lr, X_HEADS * X_HEAD_DIM, D_MODEL), (X_HEADS * X_HEAD_DIM) ** -0.5),
        "w_gate": nrm((Lr, D_MODEL, D_FF), D_MODEL ** -0.5),
        "w_up": nrm((Lr, D_MODEL, D_FF), D_MODEL ** -0.5),
        "w_down": nrm((Lr, D_FF, D_MODEL), D_FF ** -0.5),
        "n_pre_mix": gain((Lr, D_MODEL)),
        "n_post_mix": gain((Lr, D_MODEL)),
        "n_pre_x": gain((Lr, D_MODEL)),
        "n_mem": gain((Lr, D_MODEL)),
        "n_post_x": gain((Lr, D_MODEL)),
        "n_pre_ffn": gain((Lr, D_MODEL)),
        "n_post_ffn": gain((Lr, D_MODEL)),
    }


def reference(x, mem, w_in, ln_v_g, ln_v_b, w_s, b_s, conv_w, conv_b, w_a, b_a,
              w_x, b_x, lam, g_out_gmlp, g_out_lru, w_out, w_q, w_kv, w_o,
              w_gate, w_up, w_down, n_pre_mix, n_post_mix, n_pre_x, n_mem,
              n_post_x, n_pre_ffn, n_post_ffn):
    h = x
    for l in range(DEPTH):
        y = token_mixer(rms_norm(h, n_pre_mix[l]), w_in[l], ln_v_g[l], ln_v_b[l],
                        w_s[l], b_s[l], conv_w[l], conv_b[l], w_a[l], b_a[l],
                        w_x[l], b_x[l], lam[l], g_out_gmlp[l], g_out_lru[l], w_out[l])
        h = h + rms_norm(y, n_post_mix[l])
        y = mem_cross_attention(rms_norm(h, n_pre_x[l]), rms_norm(mem, n_mem[l]),
                                w_q[l], w_kv[l], w_o[l])
        h = h + rms_norm(y, n_post_x[l])
        y = swiglu(rms_norm(h, n_pre_ffn[l]), w_gate[l], w_up[l], w_down[l])
        h = h + rms_norm(y, n_post_ffn[l])
    return h
```

```python
import functools
import math

import jax
import jax.numpy as jnp
from jax import lax
from jax.experimental import pallas as pl
from jax.experimental.pallas import tpu as pltpu

D_MODEL = 1024
G_WIDTH = 512
R_WIDTH = 512
G_HEADS = 4
G_HEAD_DIM = 128
CHUNK = 128
R_HEADS = 8
R_HEAD_DIM = 64
CONV_W = 4
RG_C = 8.0
N_MEM = 256
X_HEADS = 4
X_HEAD_DIM = 256
D_FF = 2816
EPS = 1e-6

SUBLANES = 8
GATE_BLOCK = 256
TOKENS = 512
FF_CHUNK = 256
VMEM_LIMIT_BYTES = 56 * 1024 * 1024

BF16 = jnp.bfloat16
F32 = jnp.float32


def _rms(x, g):
    return x * lax.rsqrt(jnp.mean(x * x, axis=-1, keepdims=True) + EPS) * g


def _gelu(x):
    c = math.sqrt(2.0 / math.pi)
    return x * (0.5 * (1.0 + jnp.tanh(c * (x + 0.044715 * (x * x * x)))))


def _dot(a, b):
    return jnp.dot(a, b, preferred_element_type=F32)


def _kv_kernel(mem_ref, n_mem_ref, w_kv_ref, kt_ref, v_ref):
    m = _rms(mem_ref[...], n_mem_ref[...]).astype(BF16)
    width = X_HEADS * X_HEAD_DIM
    k = _dot(m, w_kv_ref[:, :width])
    v = _dot(m, w_kv_ref[:, width:])
    kt_ref[...] = k.T.astype(BF16)
    v_ref[...] = v.astype(BF16)


def _linear_scan(a, b, h0, a_scr, b_scr, h_scr):
    t, c = a.shape
    groups = t // SUBLANES
    a3 = a.reshape(groups, SUBLANES, c)
    b3 = b.reshape(groups, SUBLANES, c)
    row = lax.broadcasted_iota(jnp.int32, a3.shape, 1)
    shift = 1
    while shift < SUBLANES:
        keep = row >= shift
        a_prev = jnp.where(keep, pltpu.roll(a3, shift, 1), 1.0)
        b_prev = jnp.where(keep, pltpu.roll(b3, shift, 1), 0.0)
        b3 = a3 * b_prev + b3
        a3 = a3 * a_prev
        shift *= 2
    a_scr[...] = a3
    b_scr[...] = b3
    carry = h0
    for g in range(groups):
        blk = a_scr[g] * carry + b_scr[g]
        h_scr[g] = blk
        carry = blk[SUBLANES - 1:SUBLANES, :]
    return h_scr[...].reshape(t, c), carry


def _mixer_kernel(x_ref, n_pre_ref, w_in_ref, ln_g_ref, ln_b_ref, ws_ref, bs_ref,
                  conv_w_ref, conv_b_ref, w_gate_ref, b_a_ref, b_x_ref, lam_ref,
                  g_gmlp_ref, g_lru_ref, w_out_ref, n_post_ref, o_ref,
                  xr_scr, h_carry, a_scr, b_scr, h_scr):
    t = x_ref.shape[0]

    @pl.when(pl.program_id(1) == 0)
    def _():
        xr_scr[0:SUBLANES, :] = jnp.zeros((SUBLANES, R_WIDTH), F32)
        h_carry[...] = jnp.zeros_like(h_carry)

    x = x_ref[...]
    hn = _rms(x, n_pre_ref[...]).astype(BF16)

    u = _gelu(_dot(hn, w_in_ref[:, 0:G_WIDTH]))
    gv = _gelu(_dot(hn, w_in_ref[:, G_WIDTH:2 * G_WIDTH]))
    mu = jnp.mean(gv, axis=-1, keepdims=True)
    cen = gv - mu
    var = jnp.mean(cen * cen, axis=-1, keepdims=True)
    v = (cen * lax.rsqrt(var + EPS) * ln_g_ref[...] + ln_b_ref[...]).astype(BF16)
    tri_r = lax.broadcasted_iota(jnp.int32, (CHUNK, CHUNK), 0)
    tri_c = lax.broadcasted_iota(jnp.int32, (CHUNK, CHUNK), 1)
    causal = tri_r >= tri_c
    ws = [jnp.where(causal, ws_ref[hd], jnp.zeros((CHUNK, CHUNK), BF16)) for hd in range(G_HEADS)]
    bs = bs_ref[...]
    rows = []
    for c in range(t // CHUNK):
        cols = []
        for hd in range(G_HEADS):
            vb = v[c * CHUNK:(c + 1) * CHUNK, hd * G_HEAD_DIM:(hd + 1) * G_HEAD_DIM]
            cols.append(_dot(ws[hd], vb))
        rows.append(jnp.concatenate(cols, axis=1) + bs)
    s = jnp.concatenate(rows, axis=0)
    y_g = _rms(u * s, g_gmlp_ref[...]).astype(BF16)

    xr = _dot(hn, w_in_ref[:, 2 * G_WIDTH:2 * G_WIDTH + R_WIDTH])
    xr_scr[SUBLANES:SUBLANES + t, :] = xr
    cw = conv_w_ref[...]
    xc = conv_b_ref[...]
    for k in range(CONV_W):
        off = SUBLANES - (CONV_W - 1) + k
        xc = xc + xr_scr[off:off + t, :] * cw[k:k + 1, :]
    xr_scr[0:SUBLANES, :] = xr[t - SUBLANES:t, :]
    xc_b = xc.astype(BF16)
    ga, gx = [], []
    for blk in range(R_WIDTH // GATE_BLOCK):
        g2 = _dot(xc_b[:, blk * GATE_BLOCK:(blk + 1) * GATE_BLOCK], w_gate_ref[blk])
        ga.append(g2[:, :GATE_BLOCK])
        gx.append(g2[:, GATE_BLOCK:])
    r = jax.nn.sigmoid(jnp.concatenate(ga, axis=1) + b_a_ref[...])
    i = jax.nn.sigmoid(jnp.concatenate(gx, axis=1) + b_x_ref[...])
    nl = -lam_ref[...]
    softplus = jnp.maximum(nl, 0.0) + jnp.log1p(jnp.exp(-jnp.abs(nl)))
    log_a = (-RG_C) * r * softplus
    a = jnp.exp(log_a)
    mult = jnp.sqrt(-jnp.tanh(log_a) * (a * a + 1.0))
    bterm = mult * (i * xc)
    hseq, h_last = _linear_scan(a, bterm, h_carry[...], a_scr, b_scr, h_scr)
    h_carry[...] = h_last
    gr = _dot(hn, w_in_ref[:, 2 * G_WIDTH + R_WIDTH:])
    y_r = _rms(_gelu(gr) * hseq, g_lru_ref[...]).astype(BF16)

    y = _dot(jnp.concatenate([y_g, y_r], axis=1), w_out_ref[...])
    o_ref[...] = x + _rms(y, n_post_ref[...])


def _xattn_kernel(h_ref, n_pre_ref, w_q_ref, kt_ref, v_ref, w_o_ref, n_post_ref, o_ref):
    h = h_ref[...]
    hn = _rms(h, n_pre_ref[...]).astype(BF16)
    q = _dot(hn, w_q_ref[...]).astype(BF16)
    outs = []
    for hd in range(X_HEADS):
        sl = slice(hd * X_HEAD_DIM, (hd + 1) * X_HEAD_DIM)
        sc = _dot(q[:, sl], kt_ref[sl, :]) * (X_HEAD_DIM ** -0.5)
        m = jnp.max(sc, axis=-1, keepdims=True)
        e = jnp.exp(sc - m)
        p = (e / jnp.sum(e, axis=-1, keepdims=True)).astype(BF16)
        outs.append(_dot(p, v_ref[:, sl]).astype(BF16))
    y = _dot(jnp.concatenate(outs, axis=1), w_o_ref[...])
    o_ref[...] = h + _rms(y, n_post_ref[...])


def _ffn_kernel(h_ref, n_pre_ref, w_gate_ref, w_up_ref, w_down_ref, n_post_ref, o_ref, act_scr):
    h = h_ref[...]
    hn = _rms(h, n_pre_ref[...]).astype(BF16)
    for c in range(D_FF // FF_CHUNK):
        sl = slice(c * FF_CHUNK, (c + 1) * FF_CHUNK)
        g = _dot(hn, w_gate_ref[:, sl])
        up = _dot(hn, w_up_ref[:, sl])
        act_scr[:, sl] = (g * jax.nn.sigmoid(g) * up).astype(BF16)
    y = _dot(act_scr[...], w_down_ref[...])
    o_ref[...] = h + _rms(y, n_post_ref[...])


def _resident(shape):
    zeros = (0,) * len(shape)
    return pl.BlockSpec(shape, lambda *_: zeros, pipeline_mode=pl.Buffered(1))


def _token_spec(t):
    return pl.BlockSpec((None, t, D_MODEL), lambda b, s: (b, s, 0))


def _params(n_axes):
    return pltpu.CompilerParams(dimension_semantics=("arbitrary",) * n_axes,
                                vmem_limit_bytes=VMEM_LIMIT_BYTES)


def _block_diag(w):
    h, d, _ = w.shape
    eye = jnp.eye(h, dtype=w.dtype)
    return jnp.einsum('hij,hk->hikj', w, eye).reshape(h * d, h * d)


def _row(p):
    return p.reshape(1, -1).astype(F32)


def kernel(x, mem, w_in, ln_v_g, ln_v_b, w_s, b_s, conv_w, conv_b, w_a, b_a, w_x, b_x, lam,
           g_out_gmlp, g_out_lru, w_out, w_q, w_kv, w_o, w_gate, w_up, w_down, n_pre_mix,
           n_post_mix, n_pre_x, n_mem, n_post_x, n_pre_ffn, n_post_ffn):
    batch, seq, _ = x.shape
    depth = w_in.shape[0]
    t = TOKENS
    assert seq % t == 0 and t % CHUNK == 0
    grid = (batch, seq // t)
    tok = _token_spec(t)
    out_sds = jax.ShapeDtypeStruct((batch, seq, D_MODEL), F32)
    heads_per_block = GATE_BLOCK // R_HEAD_DIM

    h = x
    for l in range(depth):
        gate_blocks = []
        for blk in range(R_WIDTH // GATE_BLOCK):
            hs = slice(blk * heads_per_block, (blk + 1) * heads_per_block)
            gate_blocks.append(jnp.concatenate(
                [_block_diag(w_a[l, hs]), _block_diag(w_x[l, hs])], axis=1))
        w_gate_lru = jnp.stack(gate_blocks).astype(BF16)
        bs_tile = jnp.repeat(b_s[l].T, G_HEAD_DIM, axis=1)

        mixer_in = [
            (h, tok),
            (_row(n_pre_mix[l]), None),
            (w_in[l].astype(BF16), None),
            (_row(ln_v_g[l]), None),
            (_row(ln_v_b[l]), None),
            (w_s[l].astype(BF16), None),
            (bs_tile.astype(F32), None),
            (conv_w[l].astype(F32), None),
            (_row(conv_b[l]), None),
            (w_gate_lru, None),
            (_row(b_a[l]), None),
            (_row(b_x[l]), None),
            (_row(lam[l]), None),
            (_row(g_out_gmlp[l]), None),
            (_row(g_out_lru[l]), None),
            (w_out[l].astype(BF16), None),
            (_row(n_post_mix[l]), None),
        ]
        groups = t // SUBLANES
        h = pl.pallas_call(
            _mixer_kernel,
            grid=grid,
            in_specs=[spec if spec is not None else _resident(a.shape) for a, spec in mixer_in],
            out_specs=tok,
            out_shape=out_sds,
            scratch_shapes=[
                pltpu.VMEM((t + SUBLANES, R_WIDTH), F32),
                pltpu.VMEM((1, R_WIDTH), F32),
                pltpu.VMEM((groups, SUBLANES, R_WIDTH), F32),
                pltpu.VMEM((groups, SUBLANES, R_WIDTH), F32),
                pltpu.VMEM((groups, SUBLANES, R_WIDTH), F32),
            ],
            compiler_params=_params(2),
            name="mixer",
        )(*[a for a, _ in mixer_in])

        kv_in = [mem, _row(n_mem[l]), w_kv[l].astype(BF16)]
        kt, v = pl.pallas_call(
            _kv_kernel,
            grid=(batch,),
            in_specs=[pl.BlockSpec((None, N_MEM, D_MODEL), lambda b: (b, 0, 0)),
                      _resident(kv_in[1].shape), _resident(kv_in[2].shape)],
            out_specs=[pl.BlockSpec((None, D_MODEL, N_MEM), lambda b: (b, 0, 0)),
                       pl.BlockSpec((None, N_MEM, D_MODEL), lambda b: (b, 0, 0))],
            out_shape=[jax.ShapeDtypeStruct((batch, D_MODEL, N_MEM), BF16),
                       jax.ShapeDtypeStruct((batch, N_MEM, D_MODEL), BF16)],
            compiler_params=_params(1),
            name="kv_proj",
        )(*kv_in)

        x_in = [
            (h, tok),
            (_row(n_pre_x[l]), None),
            (w_q[l].astype(BF16), None),
            (kt, pl.BlockSpec((None, D_MODEL, N_MEM), lambda b, s: (b, 0, 0))),
            (v, pl.BlockSpec((None, N_MEM, D_MODEL), lambda b, s: (b, 0, 0))),
            (w_o[l].astype(BF16), None),
            (_row(n_post_x[l]), None),
        ]
        h = pl.pallas_call(
            _xattn_kernel,
            grid=grid,
            in_specs=[spec if spec is not None else _resident(a.shape) for a, spec in x_in],
            out_specs=tok,
            out_shape=out_sds,
            compiler_params=_params(2),
            name="xattn",
        )(*[a for a, _ in x_in])

        f_in = [
            (h, tok),
            (_row(n_pre_ffn[l]), None),
            (w_gate[l].astype(BF16), None),
            (w_up[l].astype(BF16), None),
            (w_down[l].astype(BF16), None),
            (_row(n_post_ffn[l]), None),
        ]
        h = pl.pallas_call(
            _ffn_kernel,
            grid=grid,
            in_specs=[spec if spec is not None else _resident(a.shape) for a, spec in f_in],
            out_specs=tok,
            out_shape=out_sds,
            scratch_shapes=[pltpu.VMEM((t, D_FF), BF16)],
            compiler_params=_params(2),
            name="ffn",
        )(*[a for a, _ in f_in])
    return h
```

```python
import math

import jax
import jax.numpy as jnp
from jax import lax
from jax.experimental import pallas as pl
from jax.experimental.pallas import tpu as pltpu

D_MODEL = 1024
G_WIDTH = 512
R_WIDTH = 512
G_HEADS = 4
G_HEAD_DIM = 128
CHUNK = 128
R_HEADS = 8
R_HEAD_DIM = 64
CONV_W = 4
RG_C = 8.0
N_MEM = 256
X_HEADS = 4
X_HEAD_DIM = 256
D_FF = 2816
EPS = 1e-6

SUBLANES = 8
GATE_BLOCK = 256
TOKENS = 512
FF_CHUNK = 256
DOWN_CHUNK = 512
VMEM_LIMIT_BYTES = 60 * 1024 * 1024

BF16 = jnp.bfloat16
F32 = jnp.float32


def _rms(x, g):
    return x * lax.rsqrt(jnp.mean(x * x, axis=-1, keepdims=True) + EPS) * g


def _gelu(x):
    c = math.sqrt(2.0 / math.pi)
    return x * (0.5 * (1.0 + jnp.tanh(c * (x + 0.044715 * (x * x * x)))))


def _dot(a, b):
    return jnp.dot(a, b, preferred_element_type=F32)


def _kv_kernel(mem_ref, n_mem_ref, w_kv_ref, kt_ref, v_ref):
    m = _rms(mem_ref[...], n_mem_ref[...]).astype(BF16)
    width = X_HEADS * X_HEAD_DIM
    k = _dot(m, w_kv_ref[:, :width])
    v = _dot(m, w_kv_ref[:, width:])
    kt_ref[...] = k.T.astype(BF16)
    v_ref[...] = v.astype(BF16)


def _linear_scan(a, b, h0, a_scr, b_scr, h_scr):
    t, c = a.shape
    groups = t // SUBLANES
    a3 = a.reshape(groups, SUBLANES, c)
    b3 = b.reshape(groups, SUBLANES, c)
    row = lax.broadcasted_iota(jnp.int32, a3.shape, 1)
    shift = 1
    while shift < SUBLANES:
        keep = row >= shift
        a_prev = jnp.where(keep, pltpu.roll(a3, shift, 1), 1.0)
        b_prev = jnp.where(keep, pltpu.roll(b3, shift, 1), 0.0)
        b3 = a3 * b_prev + b3
        a3 = a3 * a_prev
        shift *= 2
    a_scr[...] = a3
    b_scr[...] = b3
    carry = h0
    for g in range(groups):
        blk = a_scr[g] * carry + b_scr[g]
        h_scr[g] = blk
        carry = blk[SUBLANES - 1:SUBLANES, :]
    return h_scr[...].reshape(t, c), carry


def _stage_a(x_ref, p, h_buf, xr_scr, h_carry, a_scr, b_scr, h_scr):
    x = x_ref[...]
    t = x.shape[0]
    w_in_ref = p["w_in"]
    hn = _rms(x, p["n_pre_mix"][...]).astype(BF16)
    yield

    u = _gelu(_dot(hn, w_in_ref[:, 0:G_WIDTH]))
    yield
    gv = _gelu(_dot(hn, w_in_ref[:, G_WIDTH:2 * G_WIDTH]))
    mu = jnp.mean(gv, axis=-1, keepdims=True)
    cen = gv - mu
    var = jnp.mean(cen * cen, axis=-1, keepdims=True)
    v = (cen * lax.rsqrt(var + EPS) * p["ln_v_g"][...] + p["ln_v_b"][...]).astype(BF16)
    yield
    tri_r = lax.broadcasted_iota(jnp.int32, (CHUNK, CHUNK), 0)
    tri_c = lax.broadcasted_iota(jnp.int32, (CHUNK, CHUNK), 1)
    causal = tri_r >= tri_c
    ws = [jnp.where(causal, p["w_s"][hd], jnp.zeros((CHUNK, CHUNK), BF16)) for hd in range(G_HEADS)]
    bs = p["b_s"][...]
    rows = []
    for c in range(t // CHUNK):
        cols = []
        for hd in range(G_HEADS):
            vb = v[c * CHUNK:(c + 1) * CHUNK, hd * G_HEAD_DIM:(hd + 1) * G_HEAD_DIM]
            cols.append(_dot(ws[hd], vb))
        rows.append(jnp.concatenate(cols, axis=1) + bs)
    s = jnp.concatenate(rows, axis=0)
    y_g = _rms(u * s, p["g_out_gmlp"][...]).astype(BF16)
    yield

    xr = _dot(hn, w_in_ref[:, 2 * G_WIDTH:2 * G_WIDTH + R_WIDTH])
    xr_scr[SUBLANES:SUBLANES + t, :] = xr
    cw = p["conv_w"][...]
    xc = p["conv_b"][...]
    for k in range(CONV_W):
        off = SUBLANES - (CONV_W - 1) + k
        xc = xc + xr_scr[off:off + t, :] * cw[k:k + 1, :]
    xr_scr[0:SUBLANES, :] = xr[t - SUBLANES:t, :]
    xc_b = xc.astype(BF16)
    yield
    ga, gx = [], []
    for blk in range(R_WIDTH // GATE_BLOCK):
        g2 = _dot(xc_b[:, blk * GATE_BLOCK:(blk + 1) * GATE_BLOCK], p["w_gate_lru"][blk])
        ga.append(g2[:, :GATE_BLOCK])
        gx.append(g2[:, GATE_BLOCK:])
    r = jax.nn.sigmoid(jnp.concatenate(ga, axis=1) + p["b_a"][...])
    i = jax.nn.sigmoid(jnp.concatenate(gx, axis=1) + p["b_x"][...])
    nl = -p["lam"][...]
    softplus = jnp.maximum(nl, 0.0) + jnp.log1p(jnp.exp(-jnp.abs(nl)))
    log_a = (-RG_C) * r * softplus
    a = jnp.exp(log_a)
    mult = jnp.sqrt(-jnp.tanh(log_a) * (a * a + 1.0))
    bterm = mult * (i * xc)
    yield
    hseq, h_last = _linear_scan(a, bterm, h_carry[...], a_scr, b_scr, h_scr)
    h_carry[...] = h_last
    yield
    gr = _dot(hn, w_in_ref[:, 2 * G_WIDTH + R_WIDTH:])
    y_r = _rms(_gelu(gr) * hseq, p["g_out_lru"][...]).astype(BF16)
    yield
    y = _dot(jnp.concatenate([y_g, y_r], axis=1), p["w_out"][...])
    h1 = x + _rms(y, p["n_post_mix"][...])
    yield

    hn = _rms(h1, p["n_pre_x"][...]).astype(BF16)
    q = _dot(hn, p["w_q"][...]).astype(BF16)
    yield
    outs = []
    for hd in range(X_HEADS):
        sl = slice(hd * X_HEAD_DIM, (hd + 1) * X_HEAD_DIM)
        sc = _dot(q[:, sl], p["kt"][sl, :]) * (X_HEAD_DIM ** -0.5)
        m = jnp.max(sc, axis=-1, keepdims=True)
        e = jnp.exp(sc - m)
        pr = (e / jnp.sum(e, axis=-1, keepdims=True)).astype(BF16)
        outs.append(_dot(pr, p["v"][:, sl]).astype(BF16))
        yield
    y = _dot(jnp.concatenate(outs, axis=1), p["w_o"][...])
    h2 = h1 + _rms(y, p["n_post_x"][...])
    yield _AFTER_OTHERS
    h_buf[...] = h2


def _stage_b(h_buf, p, act_scr, o_ref):
    hn = _rms(h_buf[...], p["n_pre_ffn"][...]).astype(BF16)
    yield
    for c in range(D_FF // FF_CHUNK):
        sl = slice(c * FF_CHUNK, (c + 1) * FF_CHUNK)
        g = _dot(hn, p["w_gate"][:, sl])
        up = _dot(hn, p["w_up"][:, sl])
        act_scr[:, sl] = (g * jax.nn.sigmoid(g) * up).astype(BF16)
        yield
    ys = []
    for n in range(D_MODEL // DOWN_CHUNK):
        ys.append(_dot(act_scr[...], p["w_down"][:, n * DOWN_CHUNK:(n + 1) * DOWN_CHUNK]))
        yield
    y = jnp.concatenate(ys, axis=1)
    o_ref[...] = h_buf[...] + _rms(y, p["n_post_ffn"][...])


_AFTER_OTHERS = object()


def _interleave(first, second):
    live = [first, second]
    waiting = []
    while live:
        for gen in list(live):
            try:
                if next(gen) is _AFTER_OTHERS:
                    live.remove(gen)
                    waiting.append(gen)
            except StopIteration:
                live.remove(gen)
    for gen in waiting:
        for _ in gen:
            pass


_PARAM_ORDER = (
    "n_pre_mix", "w_in", "ln_v_g", "ln_v_b", "w_s", "b_s", "conv_w", "conv_b", "w_gate_lru",
    "b_a", "b_x", "lam", "g_out_gmlp", "g_out_lru", "w_out", "n_post_mix",
    "n_pre_x", "w_q", "kt", "v", "w_o", "n_post_x",
    "n_pre_ffn", "w_gate", "w_up", "w_down", "n_post_ffn",
)


def _layer_kernel(tiles_per_seq, x_ref, *rest):
    n_p = len(_PARAM_ORDER)
    p = dict(zip(_PARAM_ORDER, rest[:n_p]))
    o_ref = rest[n_p]
    h_buf, act_scr, xr_scr, h_carry, a_scr, b_scr, h_scr = rest[n_p + 1:]
    step = pl.program_id(0)
    n_tiles = pl.num_programs(0) - 1
    tile = jnp.minimum(step, n_tiles - 1)

    @pl.when(step == 0)
    def _():
        h_buf[...] = jnp.zeros_like(h_buf)

    @pl.when(lax.rem(tile, tiles_per_seq) == 0)
    def _():
        xr_scr[0:SUBLANES, :] = jnp.zeros((SUBLANES, R_WIDTH), F32)
        h_carry[...] = jnp.zeros_like(h_carry)

    stage_b = _stage_b(h_buf, p, act_scr, o_ref)
    stage_a = _stage_a(x_ref, p, h_buf, xr_scr, h_carry, a_scr, b_scr, h_scr)
    _interleave(stage_b, stage_a)


def _resident(shape):
    zeros = (0,) * len(shape)
    return pl.BlockSpec(shape, lambda *_: zeros, pipeline_mode=pl.Buffered(1))


def _block_diag(w):
    h, d, _ = w.shape
    eye = jnp.eye(h, dtype=w.dtype)
    return jnp.einsum('hij,hk->hikj', w, eye).reshape(h * d, h * d)


def _row(p):
    return p.reshape(1, -1).astype(F32)


def kernel(x, mem, w_in, ln_v_g, ln_v_b, w_s, b_s, conv_w, conv_b, w_a, b_a, w_x, b_x, lam,
           g_out_gmlp, g_out_lru, w_out, w_q, w_kv, w_o, w_gate, w_up, w_down, n_pre_mix,
           n_post_mix, n_pre_x, n_mem, n_post_x, n_pre_ffn, n_post_ffn):
    batch, seq, _ = x.shape
    depth = w_in.shape[0]
    t = TOKENS
    assert seq % t == 0 and t % CHUNK == 0
    tiles_per_seq = seq // t
    n_tiles = batch * tiles_per_seq
    heads_per_block = GATE_BLOCK // R_HEAD_DIM
    groups = t // SUBLANES

    def tile_of_a(i):
        return jnp.minimum(i, n_tiles - 1)

    def tile_of_b(i):
        return jnp.maximum(i - 1, 0)

    x_spec = pl.BlockSpec((None, t, D_MODEL),
                          lambda i: (tile_of_a(i) // tiles_per_seq, tile_of_a(i) % tiles_per_seq, 0))
    o_spec = pl.BlockSpec((None, t, D_MODEL),
                          lambda i: (tile_of_b(i) // tiles_per_seq, tile_of_b(i) % tiles_per_seq, 0))
    kt_spec = pl.BlockSpec((None, D_MODEL, N_MEM), lambda i: (tile_of_a(i) // tiles_per_seq, 0, 0))
    v_spec = pl.BlockSpec((None, N_MEM, D_MODEL), lambda i: (tile_of_a(i) // tiles_per_seq, 0, 0))

    h = x
    for l in range(depth):
        kv_in = [mem, _row(n_mem[l]), w_kv[l].astype(BF16)]
        kt, v = pl.pallas_call(
            _kv_kernel,
            grid=(batch,),
            in_specs=[pl.BlockSpec((None, N_MEM, D_MODEL), lambda b: (b, 0, 0)),
                      _resident(kv_in[1].shape), _resident(kv_in[2].shape)],
            out_specs=[pl.BlockSpec((None, D_MODEL, N_MEM), lambda b: (b, 0, 0)),
                       pl.BlockSpec((None, N_MEM, D_MODEL), lambda b: (b, 0, 0))],
            out_shape=[jax.ShapeDtypeStruct((batch, D_MODEL, N_MEM), BF16),
                       jax.ShapeDtypeStruct((batch, N_MEM, D_MODEL), BF16)],
            compiler_params=pltpu.CompilerParams(dimension_semantics=("arbitrary",),
                                                 vmem_limit_bytes=VMEM_LIMIT_BYTES),
            name="kv_proj",
        )(*kv_in)

        gate_blocks = []
        for blk in range(R_WIDTH // GATE_BLOCK):
            hs = slice(blk * heads_per_block, (blk + 1) * heads_per_block)
            gate_blocks.append(jnp.concatenate(
                [_block_diag(w_a[l, hs]), _block_diag(w_x[l, hs])], axis=1))

        params = {
            "n_pre_mix": _row(n_pre_mix[l]),
            "w_in": w_in[l].astype(BF16),
            "ln_v_g": _row(ln_v_g[l]),
            "ln_v_b": _row(ln_v_b[l]),
            "w_s": w_s[l].astype(BF16),
            "b_s": jnp.repeat(b_s[l].T, G_HEAD_DIM, axis=1).astype(F32),
            "conv_w": conv_w[l].astype(F32),
            "conv_b": _row(conv_b[l]),
            "w_gate_lru": jnp.stack(gate_blocks).astype(BF16),
            "b_a": _row(b_a[l]),
            "b_x": _row(b_x[l]),
            "lam": _row(lam[l]),
            "g_out_gmlp": _row(g_out_gmlp[l]),
            "g_out_lru": _row(g_out_lru[l]),
            "w_out": w_out[l].astype(BF16),
            "n_post_mix": _row(n_post_mix[l]),
            "n_pre_x": _row(n_pre_x[l]),
            "w_q": w_q[l].astype(BF16),
            "kt": kt,
            "v": v,
            "w_o": w_o[l].astype(BF16),
            "n_post_x": _row(n_post_x[l]),
            "n_pre_ffn": _row(n_pre_ffn[l]),
            "w_gate": w_gate[l].astype(BF16),
            "w_up": w_up[l].astype(BF16),
            "w_down": w_down[l].astype(BF16),
            "n_post_ffn": _row(n_post_ffn[l]),
        }
        specs = {"kt": kt_spec, "v": v_spec}
        in_specs = [x_spec] + [specs.get(k) or _resident(params[k].shape) for k in _PARAM_ORDER]

        def layer_body(*refs):
            _layer_kernel(tiles_per_seq, *refs)

        h = pl.pallas_call(
            layer_body,
            grid=(n_tiles + 1,),
            in_specs=in_specs,
            out_specs=o_spec,
            out_shape=jax.ShapeDtypeStruct((batch, seq, D_MODEL), F32),
            scratch_shapes=[
                pltpu.VMEM((t, D_MODEL), F32),
                pltpu.VMEM((t, D_FF), BF16),
                pltpu.VMEM((t + SUBLANES, R_WIDTH), F32),
                pltpu.VMEM((1, R_WIDTH), F32),
                pltpu.VMEM((groups, SUBLANES, R_WIDTH), F32),
                pltpu.VMEM((groups, SUBLANES, R_WIDTH), F32),
                pltpu.VMEM((groups, SUBLANES, R_WIDTH), F32),
            ],
            compiler_params=pltpu.CompilerParams(dimension_semantics=("arbitrary",),
                                                 vmem_limit_bytes=VMEM_LIMIT_BYTES),
            name="layer",
        )(h, *[params[k] for k in _PARAM_ORDER])
    return h
```

```python
import math

import jax
import jax.numpy as jnp
from jax import lax
from jax.experimental import pallas as pl
from jax.experimental.pallas import tpu as pltpu

D_MODEL = 1024
G_WIDTH = 512
R_WIDTH = 512
G_HEADS = 4
G_HEAD_DIM = 128
CHUNK = 128
R_HEADS = 8
R_HEAD_DIM = 64
CONV_W = 4
RG_C = 8.0
N_MEM = 256
X_HEADS = 4
X_HEAD_DIM = 256
D_FF = 2816
EPS = 1e-6

SUBLANES = 8
LANES = 128
HALO = 8
SEG_PITCH = 72
NEG_LOG2E = -math.log2(math.e)
GATE_BLOCK = 256
TOKENS = 512
FF_CHUNK = 256
DOWN_CHUNK = 512
VMEM_LIMIT_BYTES = 60 * 1024 * 1024

BF16 = jnp.bfloat16
F32 = jnp.float32


def _rms(x, g):
    return x * lax.rsqrt(jnp.mean(x * x, axis=-1, keepdims=True) + EPS) * g


def _sigmoid(x):
    return 1.0 / (1.0 + jnp.exp2(x * NEG_LOG2E))


def _gelu(x):
    k1 = 2.0 * math.sqrt(2.0 / math.pi) * NEG_LOG2E
    return x / (1.0 + jnp.exp2(x * (k1 + (k1 * 0.044715) * (x * x))))


def _dot(a, b):
    return jnp.dot(a, b, preferred_element_type=F32)


def _kv_kernel(mem_ref, n_mem_ref, w_kv_ref, kt_ref, v_ref):
    m = _rms(mem_ref[...], n_mem_ref[...]).astype(BF16)
    width = X_HEADS * X_HEAD_DIM
    k = _dot(m, w_kv_ref[:, :width])
    v = _dot(m, w_kv_ref[:, width:])
    kt_ref[...] = k.T.astype(BF16)
    v_ref[...] = v.astype(BF16)


def _stage_a(x_ref, p, h_buf, x_pad, h_pad, h_carry):
    x = x_ref[...]
    t = x.shape[0]
    w_in_ref = p["w_in"]
    hn = _rms(x, p["n_pre_mix"][...]).astype(BF16)
    yield

    u = _gelu(_dot(hn, w_in_ref[:, 0:G_WIDTH]))
    yield
    gv = _gelu(_dot(hn, w_in_ref[:, G_WIDTH:2 * G_WIDTH]))
    mu = jnp.mean(gv, axis=-1, keepdims=True)
    cen = gv - mu
    var = jnp.mean(cen * cen, axis=-1, keepdims=True)
    v = (cen * lax.rsqrt(var + EPS) * p["ln_v_g"][...] + p["ln_v_b"][...]).astype(BF16)
    yield
    tri_r = lax.broadcasted_iota(jnp.int32, (CHUNK, CHUNK), 0)
    tri_c = lax.broadcasted_iota(jnp.int32, (CHUNK, CHUNK), 1)
    causal = tri_r >= tri_c
    ws = [jnp.where(causal, p["w_s"][hd], jnp.zeros((CHUNK, CHUNK), BF16)) for hd in range(G_HEADS)]
    bs = p["b_s"][...]
    rows = []
    for c in range(t // CHUNK):
        cols = []
        for hd in range(G_HEADS):
            vb = v[c * CHUNK:(c + 1) * CHUNK, hd * G_HEAD_DIM:(hd + 1) * G_HEAD_DIM]
            cols.append(_dot(ws[hd], vb))
        rows.append(jnp.concatenate(cols, axis=1) + bs)
    s = jnp.concatenate(rows, axis=0)
    y_g = _rms(u * s, p["g_out_gmlp"][...]).astype(BF16)
    yield

    seg_len = t // SUBLANES
    n_slab = R_WIDTH // LANES
    xr = _dot(hn, w_in_ref[:, 2 * G_WIDTH:2 * G_WIDTH + R_WIDTH])
    for q in range(n_slab):
        col = xr[:, q * LANES:(q + 1) * LANES]
        for k in range(SUBLANES):
            base = HALO + k * SEG_PITCH
            x_pad[q, base:base + seg_len, :] = col[k * seg_len:(k + 1) * seg_len, :]
            if k > 0:
                x_pad[q, base - HALO:base, :] = col[k * seg_len - HALO:k * seg_len, :]

    def seg_rows(j):
        return jnp.concatenate(
            [x_pad[q, pl.ds(HALO + j, SUBLANES, stride=SEG_PITCH), :] for q in range(n_slab)], axis=1)

    xs = [seg_rows(j) for j in range(-(CONV_W - 1), seg_len)]
    for q in range(n_slab):
        x_pad[q, 0:HALO, :] = xr[t - HALO:t, q * LANES:(q + 1) * LANES]
    cw = p["conv_w"][...]
    cwb = [jnp.broadcast_to(cw[k:k + 1, :], (SUBLANES, R_WIDTH)) for k in range(CONV_W)]
    cbb = jnp.broadcast_to(p["conv_b"][...], (SUBLANES, R_WIDTH))
    xc_rows = []
    for j in range(seg_len):
        acc = cbb
        for k in range(CONV_W):
            acc = acc + xs[j + k] * cwb[k]
        xc_rows.append(acc)
    xc = jnp.concatenate(xc_rows, axis=0)
    xc_b = xc.astype(BF16)
    yield
    ga, gx = [], []
    for blk in range(R_WIDTH // GATE_BLOCK):
        g2 = _dot(xc_b[:, blk * GATE_BLOCK:(blk + 1) * GATE_BLOCK], p["w_gate_lru"][blk])
        ga.append(g2[:, :GATE_BLOCK])
        gx.append(g2[:, GATE_BLOCK:])
    r = _sigmoid(jnp.concatenate(ga, axis=1) + p["b_a"][...])
    i = _sigmoid(jnp.concatenate(gx, axis=1) + p["b_x"][...])
    nl = -p["lam"][...]
    softplus = jnp.maximum(nl, 0.0) + jnp.log1p(jnp.exp(-jnp.abs(nl)))
    neg_log_a = r * (RG_C * softplus)
    a = jnp.exp2(neg_log_a * NEG_LOG2E)
    mult = jnp.sqrt(jnp.tanh(neg_log_a) * (a * a + 1.0))
    bterm = mult * (i * xc)
    yield
    a3 = a.reshape(seg_len, SUBLANES, R_WIDTH)
    b3 = bterm.reshape(seg_len, SUBLANES, R_WIDTH)
    hs, ps = [b3[0]], [a3[0]]
    for j in range(1, seg_len):
        hs.append(a3[j] * hs[-1] + b3[j])
        ps.append(a3[j] * ps[-1])
    carry = h_carry[...]
    carries = []
    for k in range(SUBLANES):
        carries.append(carry)
        carry = hs[-1][k:k + 1, :] + ps[-1][k:k + 1, :] * carry
    h_carry[...] = carry
    cvec = jnp.concatenate(carries, axis=0)
    for j in range(seg_len):
        hj = hs[j] + ps[j] * cvec
        for q in range(n_slab):
            h_pad[q, pl.ds(j, SUBLANES, stride=SEG_PITCH), :] = hj[:, q * LANES:(q + 1) * LANES]
    hseq = jnp.concatenate(
        [jnp.concatenate([h_pad[q, k * SEG_PITCH:k * SEG_PITCH + seg_len, :]
                          for k in range(SUBLANES)], axis=0) for q in range(n_slab)], axis=1)
    yield
    gr = _dot(hn, w_in_ref[:, 2 * G_WIDTH + R_WIDTH:])
    y_r = _rms(_gelu(gr) * hseq, p["g_out_lru"][...]).astype(BF16)
    yield
    y = _dot(jnp.concatenate([y_g, y_r], axis=1), p["w_out"][...])
    h1 = x + _rms(y, p["n_post_mix"][...])
    yield

    hn = _rms(h1, p["n_pre_x"][...]).astype(BF16)
    q = _dot(hn, p["w_q"][...]).astype(BF16)
    yield
    outs = []
    for hd in range(X_HEADS):
        sl = slice(hd * X_HEAD_DIM, (hd + 1) * X_HEAD_DIM)
        sc = _dot(q[:, sl], p["kt"][sl, :]) * (X_HEAD_DIM ** -0.5)
        m = jnp.max(sc, axis=-1, keepdims=True)
        e = jnp.exp(sc - m)
        pr = (e / jnp.sum(e, axis=-1, keepdims=True)).astype(BF16)
        outs.append(_dot(pr, p["v"][:, sl]).astype(BF16))
        if hd % 2 == 1:
            yield
    y = _dot(jnp.concatenate(outs, axis=1), p["w_o"][...])
    h2 = h1 + _rms(y, p["n_post_x"][...])
    yield _AFTER_OTHERS
    h_buf[...] = h2


def _stage_b(h_buf, p, act_scr, o_ref):
    hn = _rms(h_buf[...], p["n_pre_ffn"][...]).astype(BF16)
    yield
    for c in range(D_FF // FF_CHUNK):
        sl = slice(c * FF_CHUNK, (c + 1) * FF_CHUNK)
        g = _dot(hn, p["w_gate"][:, sl])
        up = _dot(hn, p["w_up"][:, sl])
        act_scr[:, sl] = (g * _sigmoid(g) * up).astype(BF16)
        yield
    ys = []
    for n in range(D_MODEL // DOWN_CHUNK):
        ys.append(_dot(act_scr[...], p["w_down"][:, n * DOWN_CHUNK:(n + 1) * DOWN_CHUNK]))
        yield
    y = jnp.concatenate(ys, axis=1)
    o_ref[...] = h_buf[...] + _rms(y, p["n_post_ffn"][...])


_AFTER_OTHERS = object()


def _interleave(first, second):
    live = [first, second]
    waiting = []
    while live:
        for gen in list(live):
            try:
                if next(gen) is _AFTER_OTHERS:
                    live.remove(gen)
                    waiting.append(gen)
            except StopIteration:
                live.remove(gen)
    for gen in waiting:
        for _ in gen:
            pass


_PARAM_ORDER = (
    "n_pre_mix", "w_in", "ln_v_g", "ln_v_b", "w_s", "b_s", "conv_w", "conv_b", "w_gate_lru",
    "b_a", "b_x", "lam", "g_out_gmlp", "g_out_lru", "w_out", "n_post_mix",
    "n_pre_x", "w_q", "kt", "v", "w_o", "n_post_x",
    "n_pre_ffn", "w_gate", "w_up", "w_down", "n_post_ffn",
)


def _layer_kernel(tiles_per_seq, x_ref, *rest):
    n_p = len(_PARAM_ORDER)
    p = dict(zip(_PARAM_ORDER, rest[:n_p]))
    o_ref = rest[n_p]
    h_buf, act_scr, x_pad, h_pad, h_carry = rest[n_p + 1:]
    step = pl.program_id(0)
    n_tiles = pl.num_programs(0) - 1
    tile = jnp.minimum(step, n_tiles - 1)

    @pl.when(step == 0)
    def _():
        h_buf[...] = jnp.zeros_like(h_buf)

    @pl.when(lax.rem(tile, tiles_per_seq) == 0)
    def _():
        x_pad[:, 0:HALO, :] = jnp.zeros((R_WIDTH // LANES, HALO, LANES), F32)
        h_carry[...] = jnp.zeros_like(h_carry)

    stage_b = _stage_b(h_buf, p, act_scr, o_ref)
    stage_a = _stage_a(x_ref, p, h_buf, x_pad, h_pad, h_carry)
    _interleave(stage_b, stage_a)


def _resident(shape):
    zeros = (0,) * len(shape)
    return pl.BlockSpec(shape, lambda *_: zeros, pipeline_mode=pl.Buffered(1))


def _block_diag(w):
    h, d, _ = w.shape
    eye = jnp.eye(h, dtype=w.dtype)
    return jnp.einsum('hij,hk->hikj', w, eye).reshape(h * d, h * d)


def _row(p):
    return p.reshape(1, -1).astype(F32)


def kernel(x, mem, w_in, ln_v_g, ln_v_b, w_s, b_s, conv_w, conv_b, w_a, b_a, w_x, b_x, lam,
           g_out_gmlp, g_out_lru, w_out, w_q, w_kv, w_o, w_gate, w_up, w_down, n_pre_mix,
           n_post_mix, n_pre_x, n_mem, n_post_x, n_pre_ffn, n_post_ffn):
    batch, seq, _ = x.shape
    depth = w_in.shape[0]
    t = TOKENS
    assert seq % t == 0 and t % CHUNK == 0
    tiles_per_seq = seq // t
    n_tiles = batch * tiles_per_seq
    heads_per_block = GATE_BLOCK // R_HEAD_DIM

    def tile_of_a(i):
        return jnp.minimum(i, n_tiles - 1)

    def tile_of_b(i):
        return jnp.maximum(i - 1, 0)

    x_spec = pl.BlockSpec((None, t, D_MODEL),
                          lambda i: (tile_of_a(i) // tiles_per_seq, tile_of_a(i) % tiles_per_seq, 0))
    o_spec = pl.BlockSpec((None, t, D_MODEL),
                          lambda i: (tile_of_b(i) // tiles_per_seq, tile_of_b(i) % tiles_per_seq, 0))
    kt_spec = pl.BlockSpec((None, D_MODEL, N_MEM), lambda i: (tile_of_a(i) // tiles_per_seq, 0, 0))
    v_spec = pl.BlockSpec((None, N_MEM, D_MODEL), lambda i: (tile_of_a(i) // tiles_per_seq, 0, 0))

    h = x
    for l in range(depth):
        kv_in = [mem, _row(n_mem[l]), w_kv[l].astype(BF16)]
        kt, v = pl.pallas_call(
            _kv_kernel,
            grid=(batch,),
            in_specs=[pl.BlockSpec((None, N_MEM, D_MODEL), lambda b: (b, 0, 0)),
                      _resident(kv_in[1].shape), _resident(kv_in[2].shape)],
            out_specs=[pl.BlockSpec((None, D_MODEL, N_MEM), lambda b: (b, 0, 0)),
                       pl.BlockSpec((None, N_MEM, D_MODEL), lambda b: (b, 0, 0))],
            out_shape=[jax.ShapeDtypeStruct((batch, D_MODEL, N_MEM), BF16),
                       jax.ShapeDtypeStruct((batch, N_MEM, D_MODEL), BF16)],
            compiler_params=pltpu.CompilerParams(dimension_semantics=("arbitrary",),
                                                 vmem_limit_bytes=VMEM_LIMIT_BYTES),
            name="kv_proj",
        )(*kv_in)

        gate_blocks = []
        for blk in range(R_WIDTH // GATE_BLOCK):
            hs = slice(blk * heads_per_block, (blk + 1) * heads_per_block)
            gate_blocks.append(jnp.concatenate(
                [_block_diag(w_a[l, hs]), _block_diag(w_x[l, hs])], axis=1))

        params = {
            "n_pre_mix": _row(n_pre_mix[l]),
            "w_in": w_in[l].astype(BF16),
            "ln_v_g": _row(ln_v_g[l]),
            "ln_v_b": _row(ln_v_b[l]),
            "w_s": w_s[l].astype(BF16),
            "b_s": jnp.repeat(b_s[l].T, G_HEAD_DIM, axis=1).astype(F32),
            "conv_w": conv_w[l].astype(F32),
            "conv_b": _row(conv_b[l]),
            "w_gate_lru": jnp.stack(gate_blocks).astype(BF16),
            "b_a": _row(b_a[l]),
            "b_x": _row(b_x[l]),
            "lam": _row(lam[l]),
            "g_out_gmlp": _row(g_out_gmlp[l]),
            "g_out_lru": _row(g_out_lru[l]),
            "w_out": w_out[l].astype(BF16),
            "n_post_mix": _row(n_post_mix[l]),
            "n_pre_x": _row(n_pre_x[l]),
            "w_q": w_q[l].astype(BF16),
            "kt": kt,
            "v": v,
            "w_o": w_o[l].astype(BF16),
            "n_post_x": _row(n_post_x[l]),
            "n_pre_ffn": _row(n_pre_ffn[l]),
            "w_gate": w_gate[l].astype(BF16),
            "w_up": w_up[l].astype(BF16),
            "w_down": w_down[l].astype(BF16),
            "n_post_ffn": _row(n_post_ffn[l]),
        }
        specs = {"kt": kt_spec, "v": v_spec}
        in_specs = [x_spec] + [specs.get(k) or _resident(params[k].shape) for k in _PARAM_ORDER]

        def layer_body(*refs):
            _layer_kernel(tiles_per_seq, *refs)

        h = pl.pallas_call(
            layer_body,
            grid=(n_tiles + 1,),
            in_specs=in_specs,
            out_specs=o_spec,
            out_shape=jax.ShapeDtypeStruct((batch, seq, D_MODEL), F32),
            scratch_shapes=[
                pltpu.VMEM((t, D_MODEL), F32),
                pltpu.VMEM((t, D_FF), BF16),
                pltpu.VMEM((R_WIDTH // LANES, HALO + SUBLANES * SEG_PITCH, LANES), F32),
                pltpu.VMEM((R_WIDTH // LANES, SUBLANES * SEG_PITCH, LANES), F32),
                pltpu.VMEM((1, R_WIDTH), F32),
            ],
            compiler_params=pltpu.CompilerParams(dimension_semantics=("arbitrary",),
                                                 vmem_limit_bytes=VMEM_LIMIT_BYTES),
            name="layer",
        )(h, *[params[k] for k in _PARAM_ORDER])
    return h
```

```python
import math

import jax
import jax.numpy as jnp
from jax import lax
from jax.experimental import pallas as pl
from jax.experimental.pallas import tpu as pltpu

D_MODEL = 1024
G_WIDTH = 512
R_WIDTH = 512
G_HEADS = 4
G_HEAD_DIM = 128
CHUNK = 128
R_HEADS = 8
R_HEAD_DIM = 64
CONV_W = 4
RG_C = 8.0
N_MEM = 256
X_HEADS = 4
X_HEAD_DIM = 256
D_FF = 2816
EPS = 1e-6

SUBLANES = 8
LANES = 128
HALO = 8
SEG_PITCH = 72
NEG_LOG2E = -math.log2(math.e)
GATE_BLOCK = 256
TOKENS = 512
FF_CHUNK = 256
DOWN_CHUNK = 512
VMEM_LIMIT_BYTES = 60 * 1024 * 1024

BF16 = jnp.bfloat16
F32 = jnp.float32


def _rms(x, g):
    return x * lax.rsqrt(jnp.mean(x * x, axis=-1, keepdims=True) + EPS) * g


def _sigmoid(x):
    return jax.nn.sigmoid(x)


def _gelu(x):
    c = math.sqrt(2.0 / math.pi)
    return x * (0.5 * (1.0 + jnp.tanh(c * (x + 0.044715 * (x * x * x)))))


def _dot(a, b):
    return jnp.dot(a, b, preferred_element_type=F32)


def _kv_kernel(mem_ref, n_mem_ref, w_kv_ref, kt_ref, v_ref):
    m = _rms(mem_ref[...], n_mem_ref[...]).astype(BF16)
    width = X_HEADS * X_HEAD_DIM
    k = _dot(m, w_kv_ref[:, :width])
    v = _dot(m, w_kv_ref[:, width:])
    kt_ref[...] = k.T.astype(BF16)
    v_ref[...] = v.astype(BF16)


def _stage_a(x_ref, p, h_buf, x_pad, h_pad, h_carry):
    x = x_ref[...]
    t = x.shape[0]
    w_in_ref = p["w_in"]
    hn = _rms(x, p["n_pre_mix"][...]).astype(BF16)
    yield

    u = _gelu(_dot(hn, w_in_ref[:, 0:G_WIDTH]))
    yield
    gv = _gelu(_dot(hn, w_in_ref[:, G_WIDTH:2 * G_WIDTH]))
    mu = jnp.mean(gv, axis=-1, keepdims=True)
    cen = gv - mu
    var = jnp.mean(cen * cen, axis=-1, keepdims=True)
    v = (cen * lax.rsqrt(var + EPS) * p["ln_v_g"][...] + p["ln_v_b"][...]).astype(BF16)
    yield
    tri_r = lax.broadcasted_iota(jnp.int32, (CHUNK, CHUNK), 0)
    tri_c = lax.broadcasted_iota(jnp.int32, (CHUNK, CHUNK), 1)
    causal = tri_r >= tri_c
    ws = [jnp.where(causal, p["w_s"][hd], jnp.zeros((CHUNK, CHUNK), BF16)) for hd in range(G_HEADS)]
    bs = p["b_s"][...]
    rows = []
    for c in range(t // CHUNK):
        cols = []
        for hd in range(G_HEADS):
            vb = v[c * CHUNK:(c + 1) * CHUNK, hd * G_HEAD_DIM:(hd + 1) * G_HEAD_DIM]
            cols.append(_dot(ws[hd], vb))
        rows.append(jnp.concatenate(cols, axis=1) + bs)
    s = jnp.concatenate(rows, axis=0)
    y_g = _rms(u * s, p["g_out_gmlp"][...]).astype(BF16)
    yield

    seg_len = t // SUBLANES
    n_slab = R_WIDTH // LANES
    xr = _dot(hn, w_in_ref[:, 2 * G_WIDTH:2 * G_WIDTH + R_WIDTH])
    for q in range(n_slab):
        col = xr[:, q * LANES:(q + 1) * LANES]
        for k in range(SUBLANES):
            base = HALO + k * SEG_PITCH
            x_pad[q, base:base + seg_len, :] = col[k * seg_len:(k + 1) * seg_len, :]
            if k > 0:
                x_pad[q, base - HALO:base, :] = col[k * seg_len - HALO:k * seg_len, :]

    def seg_rows(j):
        return jnp.concatenate(
            [x_pad[q, pl.ds(HALO + j, SUBLANES, stride=SEG_PITCH), :] for q in range(n_slab)], axis=1)

    xs = [seg_rows(j) for j in range(-(CONV_W - 1), seg_len)]
    for q in range(n_slab):
        x_pad[q, 0:HALO, :] = xr[t - HALO:t, q * LANES:(q + 1) * LANES]
    cw = p["conv_w"][...]
    cwb = [jnp.broadcast_to(cw[k:k + 1, :], (SUBLANES, R_WIDTH)) for k in range(CONV_W)]
    cbb = jnp.broadcast_to(p["conv_b"][...], (SUBLANES, R_WIDTH))
    xc_rows = []
    for j in range(seg_len):
        acc = cbb
        for k in range(CONV_W):
            acc = acc + xs[j + k] * cwb[k]
        xc_rows.append(acc)
    xc = jnp.concatenate(xc_rows, axis=0)
    xc_b = xc.astype(BF16)
    yield
    ga, gx = [], []
    for blk in range(R_WIDTH // GATE_BLOCK):
        g2 = _dot(xc_b[:, blk * GATE_BLOCK:(blk + 1) * GATE_BLOCK], p["w_gate_lru"][blk])
        ga.append(g2[:, :GATE_BLOCK])
        gx.append(g2[:, GATE_BLOCK:])
    r = _sigmoid(jnp.concatenate(ga, axis=1) + p["b_a"][...])
    i = _sigmoid(jnp.concatenate(gx, axis=1) + p["b_x"][...])
    nl = -p["lam"][...]
    softplus = jnp.maximum(nl, 0.0) + jnp.log1p(jnp.exp(-jnp.abs(nl)))
    neg_log_a = r * (RG_C * softplus)
    a = jnp.exp2(neg_log_a * NEG_LOG2E)
    mult = jnp.sqrt(jnp.tanh(neg_log_a) * (a * a + 1.0))
    bterm = mult * (i * xc)
    yield
    a3 = a.reshape(seg_len, SUBLANES, R_WIDTH)
    b3 = bterm.reshape(seg_len, SUBLANES, R_WIDTH)
    hs, ps = [b3[0]], [a3[0]]
    for j in range(1, seg_len):
        hs.append(a3[j] * hs[-1] + b3[j])
        ps.append(a3[j] * ps[-1])
    carry = h_carry[...]
    carries = []
    for k in range(SUBLANES):
        carries.append(carry)
        carry = hs[-1][k:k + 1, :] + ps[-1][k:k + 1, :] * carry
    h_carry[...] = carry
    cvec = jnp.concatenate(carries, axis=0)
    for j in range(seg_len):
        hj = hs[j] + ps[j] * cvec
        for q in range(n_slab):
            h_pad[q, pl.ds(j, SUBLANES, stride=SEG_PITCH), :] = hj[:, q * LANES:(q + 1) * LANES]
    hseq = jnp.concatenate(
        [jnp.concatenate([h_pad[q, k * SEG_PITCH:k * SEG_PITCH + seg_len, :]
                          for k in range(SUBLANES)], axis=0) for q in range(n_slab)], axis=1)
    yield
    gr = _dot(hn, w_in_ref[:, 2 * G_WIDTH + R_WIDTH:])
    y_r = _rms(_gelu(gr) * hseq, p["g_out_lru"][...]).astype(BF16)
    yield
    y = _dot(jnp.concatenate([y_g, y_r], axis=1), p["w_out"][...])
    h1 = x + _rms(y, p["n_post_mix"][...])
    yield

    hn = _rms(h1, p["n_pre_x"][...]).astype(BF16)
    q = _dot(hn, p["w_q"][...]).astype(BF16)
    yield
    outs = []
    for hd in range(X_HEADS):
        sl = slice(hd * X_HEAD_DIM, (hd + 1) * X_HEAD_DIM)
        sc = _dot(q[:, sl], p["kt"][sl, :]) * (X_HEAD_DIM ** -0.5)
        m = jnp.max(sc, axis=-1, keepdims=True)
        e = jnp.exp(sc - m)
        pr = (e / jnp.sum(e, axis=-1, keepdims=True)).astype(BF16)
        outs.append(_dot(pr, p["v"][:, sl]).astype(BF16))
        if hd % 2 == 1:
            yield
    y = _dot(jnp.concatenate(outs, axis=1), p["w_o"][...])
    h2 = h1 + _rms(y, p["n_post_x"][...])
    yield _AFTER_OTHERS
    h_buf[...] = h2


def _stage_b(h_buf, p, act_scr, o_ref):
    hn = _rms(h_buf[...], p["n_pre_ffn"][...]).astype(BF16)
    yield
    for c in range(D_FF // FF_CHUNK):
        sl = slice(c * FF_CHUNK, (c + 1) * FF_CHUNK)
        g = _dot(hn, p["w_gate"][:, sl])
        up = _dot(hn, p["w_up"][:, sl])
        act_scr[:, sl] = (g * _sigmoid(g) * up).astype(BF16)
        yield
    ys = []
    for n in range(D_MODEL // DOWN_CHUNK):
        ys.append(_dot(act_scr[...], p["w_down"][:, n * DOWN_CHUNK:(n + 1) * DOWN_CHUNK]))
        yield
    y = jnp.concatenate(ys, axis=1)
    o_ref[...] = h_buf[...] + _rms(y, p["n_post_ffn"][...])


_AFTER_OTHERS = object()


def _interleave(first, second):
    live = [first, second]
    waiting = []
    while live:
        for gen in list(live):
            try:
                if next(gen) is _AFTER_OTHERS:
                    live.remove(gen)
                    waiting.append(gen)
            except StopIteration:
                live.remove(gen)
    for gen in waiting:
        for _ in gen:
            pass


_PARAM_ORDER = (
    "n_pre_mix", "w_in", "ln_v_g", "ln_v_b", "w_s", "b_s", "conv_w", "conv_b", "w_gate_lru",
    "b_a", "b_x", "lam", "g_out_gmlp", "g_out_lru", "w_out", "n_post_mix",
    "n_pre_x", "w_q", "kt", "v", "w_o", "n_post_x",
    "n_pre_ffn", "w_gate", "w_up", "w_down", "n_post_ffn",
)


def _layer_kernel(tiles_per_seq, x_ref, *rest):
    n_p = len(_PARAM_ORDER)
    p = dict(zip(_PARAM_ORDER, rest[:n_p]))
    o_ref = rest[n_p]
    h_buf, act_scr, x_pad, h_pad, h_carry = rest[n_p + 1:]
    step = pl.program_id(0)
    n_tiles = pl.num_programs(0) - 1
    tile = jnp.minimum(step, n_tiles - 1)

    @pl.when(step == 0)
    def _():
        h_buf[...] = jnp.zeros_like(h_buf)

    @pl.when(lax.rem(tile, tiles_per_seq) == 0)
    def _():
        x_pad[:, 0:HALO, :] = jnp.zeros((R_WIDTH // LANES, HALO, LANES), F32)
        h_carry[...] = jnp.zeros_like(h_carry)

    stage_b = _stage_b(h_buf, p, act_scr, o_ref)
    stage_a = _stage_a(x_ref, p, h_buf, x_pad, h_pad, h_carry)
    _interleave(stage_b, stage_a)


def _resident(shape):
    zeros = (0,) * len(shape)
    return pl.BlockSpec(shape, lambda *_: zeros, pipeline_mode=pl.Buffered(1))


def _block_diag(w):
    h, d, _ = w.shape
    eye = jnp.eye(h, dtype=w.dtype)
    return jnp.einsum('hij,hk->hikj', w, eye).reshape(h * d, h * d)


def _row(p):
    return p.reshape(1, -1).astype(F32)


def kernel(x, mem, w_in, ln_v_g, ln_v_b, w_s, b_s, conv_w, conv_b, w_a, b_a, w_x, b_x, lam,
           g_out_gmlp, g_out_lru, w_out, w_q, w_kv, w_o, w_gate, w_up, w_down, n_pre_mix,
           n_post_mix, n_pre_x, n_mem, n_post_x, n_pre_ffn, n_post_ffn):
    batch, seq, _ = x.shape
    depth = w_in.shape[0]
    t = TOKENS
    assert seq % t == 0 and t % CHUNK == 0
    tiles_per_seq = seq // t
    n_tiles = batch * tiles_per_seq
    heads_per_block = GATE_BLOCK // R_HEAD_DIM

    def tile_of_a(i):
        return jnp.minimum(i, n_tiles - 1)

    def tile_of_b(i):
        return jnp.maximum(i - 1, 0)

    x_spec = pl.BlockSpec((None, t, D_MODEL),
                          lambda i: (tile_of_a(i) // tiles_per_seq, tile_of_a(i) % tiles_per_seq, 0))
    o_spec = pl.BlockSpec((None, t, D_MODEL),
                          lambda i: (tile_of_b(i) // tiles_per_seq, tile_of_b(i) % tiles_per_seq, 0))
    kt_spec = pl.BlockSpec((None, D_MODEL, N_MEM), lambda i: (tile_of_a(i) // tiles_per_seq, 0, 0))
    v_spec = pl.BlockSpec((None, N_MEM, D_MODEL), lambda i: (tile_of_a(i) // tiles_per_seq, 0, 0))

    h = x
    for l in range(depth):
        kv_in = [mem, _row(n_mem[l]), w_kv[l].astype(BF16)]
        kt, v = pl.pallas_call(
            _kv_kernel,
            grid=(batch,),
            in_specs=[pl.BlockSpec((None, N_MEM, D_MODEL), lambda b: (b, 0, 0)),
                      _resident(kv_in[1].shape), _resident(kv_in[2].shape)],
            out_specs=[pl.BlockSpec((None, D_MODEL, N_MEM), lambda b: (b, 0, 0)),
                       pl.BlockSpec((None, N_MEM, D_MODEL), lambda b: (b, 0, 0))],
            out_shape=[jax.ShapeDtypeStruct((batch, D_MODEL, N_MEM), BF16),
                       jax.ShapeDtypeStruct((batch, N_MEM, D_MODEL), BF16)],
            compiler_params=pltpu.CompilerParams(dimension_semantics=("arbitrary",),
                                                 vmem_limit_bytes=VMEM_LIMIT_BYTES),
            name="kv_proj",
        )(*kv_in)

        gate_blocks = []
        for blk in range(R_WIDTH // GATE_BLOCK):
            hs = slice(blk * heads_per_block, (blk + 1) * heads_per_block)
            gate_blocks.append(jnp.concatenate(
                [_block_diag(w_a[l, hs]), _block_diag(w_x[l, hs])], axis=1))

        params = {
            "n_pre_mix": _row(n_pre_mix[l]),
            "w_in": w_in[l].astype(BF16),
            "ln_v_g": _row(ln_v_g[l]),
            "ln_v_b": _row(ln_v_b[l]),
            "w_s": w_s[l].astype(BF16),
            "b_s": jnp.repeat(b_s[l].T, G_HEAD_DIM, axis=1).astype(F32),
            "conv_w": conv_w[l].astype(F32),
            "conv_b": _row(conv_b[l]),
            "w_gate_lru": jnp.stack(gate_blocks).astype(BF16),
            "b_a": _row(b_a[l]),
            "b_x": _row(b_x[l]),
            "lam": _row(lam[l]),
            "g_out_gmlp": _row(g_out_gmlp[l]),
            "g_out_lru": _row(g_out_lru[l]),
            "w_out": w_out[l].astype(BF16),
            "n_post_mix": _row(n_post_mix[l]),
            "n_pre_x": _row(n_pre_x[l]),
            "w_q": w_q[l].astype(BF16),
            "kt": kt,
            "v": v,
            "w_o": w_o[l].astype(BF16),
            "n_post_x": _row(n_post_x[l]),
            "n_pre_ffn": _row(n_pre_ffn[l]),
            "w_gate": w_gate[l].astype(BF16),
            "w_up": w_up[l].astype(BF16),
            "w_down": w_down[l].astype(BF16),
            "n_post_ffn": _row(n_post_ffn[l]),
        }
        specs = {"kt": kt_spec, "v": v_spec}
        in_specs = [x_spec] + [specs.get(k) or _resident(params[k].shape) for k in _PARAM_ORDER]

        def layer_body(*refs):
            _layer_kernel(tiles_per_seq, *refs)

        h = pl.pallas_call(
            layer_body,
            grid=(n_tiles + 1,),
            in_specs=in_specs,
            out_specs=o_spec,
            out_shape=jax.ShapeDtypeStruct((batch, seq, D_MODEL), F32),
            scratch_shapes=[
                pltpu.VMEM((t, D_MODEL), F32),
                pltpu.VMEM((t, D_FF), BF16),
                pltpu.VMEM((R_WIDTH // LANES, HALO + SUBLANES * SEG_PITCH, LANES), F32),
                pltpu.VMEM((R_WIDTH // LANES, SUBLANES * SEG_PITCH, LANES), F32),
                pltpu.VMEM((1, R_WIDTH), F32),
            ],
            compiler_params=pltpu.CompilerParams(dimension_semantics=("arbitrary",),
                                                 vmem_limit_bytes=VMEM_LIMIT_BYTES),
            name="layer",
        )(h, *[params[k] for k in _PARAM_ORDER])
    return h
```

```python
import math

import jax
import jax.numpy as jnp
from jax import lax
from jax.experimental import pallas as pl
from jax.experimental.pallas import tpu as pltpu

D_MODEL = 1024
G_WIDTH = 512
R_WIDTH = 512
G_HEADS = 4
G_HEAD_DIM = 128
CHUNK = 128
R_HEADS = 8
R_HEAD_DIM = 64
CONV_W = 4
RG_C = 8.0
N_MEM = 256
X_HEADS = 4
X_HEAD_DIM = 256
D_FF = 2816
EPS = 1e-6

SUBLANES = 8
NEG_LOG2E = -math.log2(math.e)
GATE_BLOCK = 256
TOKENS = 512
FF_CHUNK = 256
DOWN_ROW_BLOCKS = 2
DOWN_COL_BLOCKS = 2
PIPELINE_DEPTH = 2
VMEM_LIMIT_BYTES = 60 * 1024 * 1024

BF16 = jnp.bfloat16
F32 = jnp.float32


def _rms(x, g):
    return x * lax.rsqrt(jnp.mean(x * x, axis=-1, keepdims=True) + EPS) * g


def _sigmoid(x):
    return 1.0 / (1.0 + jnp.exp2(x * NEG_LOG2E))


def _gelu(x):
    k1 = 2.0 * math.sqrt(2.0 / math.pi) * NEG_LOG2E
    return x / (1.0 + jnp.exp2(x * (k1 + (k1 * 0.044715) * (x * x))))


def _dot(a, b):
    return jnp.dot(a, b, preferred_element_type=F32)


def _kv_kernel(mem_ref, n_mem_ref, w_kv_ref, kt_ref, v_ref):
    m = _rms(mem_ref[...], n_mem_ref[...]).astype(BF16)
    width = X_HEADS * X_HEAD_DIM
    k = _dot(m, w_kv_ref[:, :width])
    v = _dot(m, w_kv_ref[:, width:])
    kt_ref[...] = k.T.astype(BF16)
    v_ref[...] = v.astype(BF16)


def _exchange(blocks):
    sub = lax.broadcasted_iota(jnp.int32, blocks[0].shape, 0)
    v = list(blocks)
    d = SUBLANES // 2
    while d >= 1:
        low = (sub & d) == 0
        nxt = list(v)
        for k in range(SUBLANES):
            if k & d == 0:
                nxt[k] = jnp.where(low, v[k], pltpu.roll(v[k + d], d, 0))
                nxt[k + d] = jnp.where(low, pltpu.roll(v[k], SUBLANES - d, 0), v[k + d])
        v = nxt
        d //= 2
    return v


def _to_segment_layout(x):
    seg_len = x.shape[0] // SUBLANES
    out = []
    for m in range(seg_len // SUBLANES):
        rows = [x[k * seg_len + m * SUBLANES:k * seg_len + (m + 1) * SUBLANES, :]
                for k in range(SUBLANES)]
        out.extend(_exchange(rows))
    return out


def _from_segment_layout(xs):
    seg_len = len(xs)
    groups = [_exchange(xs[m * SUBLANES:(m + 1) * SUBLANES]) for m in range(seg_len // SUBLANES)]
    return jnp.concatenate([groups[m][k] for k in range(SUBLANES)
                            for m in range(seg_len // SUBLANES)], axis=0)


def _stage_a(x_ref, p, h_slot, hn_buf, h_carry, x_tail):
    yield
    x = x_ref[...]
    t = x.shape[0]
    w_in_ref = p["w_in"]
    hn = _rms(x, p["n_pre_mix"][...]).astype(BF16)

    yield
    u = _gelu(_dot(hn, w_in_ref[:, 0:G_WIDTH]))
    yield
    gv = _gelu(_dot(hn, w_in_ref[:, G_WIDTH:2 * G_WIDTH]))
    mu = jnp.mean(gv, axis=-1, keepdims=True)
    cen = gv - mu
    var = jnp.mean(cen * cen, axis=-1, keepdims=True)
    v = (cen * lax.rsqrt(var + EPS) * p["ln_v_g"][...] + p["ln_v_b"][...]).astype(BF16)
    yield
    tri_r = lax.broadcasted_iota(jnp.int32, (CHUNK, CHUNK), 0)
    tri_c = lax.broadcasted_iota(jnp.int32, (CHUNK, CHUNK), 1)
    causal = tri_r >= tri_c
    ws = [jnp.where(causal, p["w_s"][hd], jnp.zeros((CHUNK, CHUNK), BF16)) for hd in range(G_HEADS)]
    bs = p["b_s"][...]
    rows = []
    for c in range(t // CHUNK):
        cols = []
        for hd in range(G_HEADS):
            vb = v[c * CHUNK:(c + 1) * CHUNK, hd * G_HEAD_DIM:(hd + 1) * G_HEAD_DIM]
            cols.append(_dot(ws[hd], vb))
        rows.append(jnp.concatenate(cols, axis=1) + bs)
    s = jnp.concatenate(rows, axis=0)
    y_g = _rms(u * s, p["g_out_gmlp"][...]).astype(BF16)

    yield
    seg_len = t // SUBLANES
    xr = _dot(hn, w_in_ref[:, 2 * G_WIDTH:2 * G_WIDTH + R_WIDTH])
    xs = _to_segment_layout(xr)
    sub = lax.broadcasted_iota(jnp.int32, (SUBLANES, R_WIDTH), 0)
    halo = [jnp.where(sub == 0, pltpu.roll(x_tail[m], 1, 0),
                      pltpu.roll(xs[seg_len - (CONV_W - 1) + m], 1, 0)) for m in range(CONV_W - 1)]
    for m in range(CONV_W - 1):
        x_tail[m] = xs[seg_len - (CONV_W - 1) + m]
    xs = halo + xs
    cw = p["conv_w"][...]
    cwb = [jnp.broadcast_to(cw[k:k + 1, :], (SUBLANES, R_WIDTH)) for k in range(CONV_W)]
    cbb = jnp.broadcast_to(p["conv_b"][...], (SUBLANES, R_WIDTH))
    xc_rows = []
    for j in range(seg_len):
        acc = cbb
        for k in range(CONV_W):
            acc = acc + xs[j + k] * cwb[k]
        xc_rows.append(acc)
    xc = jnp.concatenate(xc_rows, axis=0)
    xc_b = xc.astype(BF16)
    yield
    ga, gx = [], []
    for blk in range(R_WIDTH // GATE_BLOCK):
        g2 = _dot(xc_b[:, blk * GATE_BLOCK:(blk + 1) * GATE_BLOCK], p["w_gate_lru"][blk])
        ga.append(g2[:, :GATE_BLOCK])
        gx.append(g2[:, GATE_BLOCK:])
    r = _sigmoid(jnp.concatenate(ga, axis=1) + p["b_a"][...])
    i = _sigmoid(jnp.concatenate(gx, axis=1) + p["b_x"][...])
    nl = -p["lam"][...]
    softplus = jnp.maximum(nl, 0.0) + jnp.log1p(jnp.exp(-jnp.abs(nl)))
    neg_log_a = r * (RG_C * softplus)
    a = jnp.exp2(neg_log_a * NEG_LOG2E)
    mult = jnp.sqrt(jnp.tanh(neg_log_a) * (a * a + 1.0))
    bterm = mult * (i * xc)
    yield
    a3 = a.reshape(seg_len, SUBLANES, R_WIDTH)
    b3 = bterm.reshape(seg_len, SUBLANES, R_WIDTH)
    hs, ps = [b3[0]], [a3[0]]
    for j in range(1, seg_len):
        hs.append(a3[j] * hs[-1] + b3[j])
        ps.append(a3[j] * ps[-1])
    carry = h_carry[...]
    carries = []
    for k in range(SUBLANES):
        carries.append(carry)
        carry = hs[-1][k:k + 1, :] + ps[-1][k:k + 1, :] * carry
    h_carry[...] = carry
    cvec = jnp.concatenate(carries, axis=0)
    hseq = _from_segment_layout([hs[j] + ps[j] * cvec for j in range(seg_len)])
    yield
    gr = _dot(hn, w_in_ref[:, 2 * G_WIDTH + R_WIDTH:])
    y_r = _rms(_gelu(gr) * hseq, p["g_out_lru"][...]).astype(BF16)
    yield
    y = _dot(jnp.concatenate([y_g, y_r], axis=1), p["w_out"][...])
    h1 = x + _rms(y, p["n_post_mix"][...])

    yield
    hn = _rms(h1, p["n_pre_x"][...]).astype(BF16)
    q = _dot(hn, p["w_q"][...]).astype(BF16)
    outs = []
    for hd in range(X_HEADS):
        if hd % 2 == 0:
            yield
        sl = slice(hd * X_HEAD_DIM, (hd + 1) * X_HEAD_DIM)
        sc = _dot(q[:, sl], p["kt"][sl, :]) * (X_HEAD_DIM ** -0.5)
        m = jnp.max(sc, axis=-1, keepdims=True)
        e = jnp.exp(sc - m)
        pr = (e / jnp.sum(e, axis=-1, keepdims=True)).astype(BF16)
        outs.append(_dot(pr, p["v"][:, sl]).astype(BF16))
    yield
    y = _dot(jnp.concatenate(outs, axis=1), p["w_o"][...])
    h2 = h1 + _rms(y, p["n_post_x"][...])
    hn2 = _rms(h2, p["n_pre_ffn"][...]).astype(BF16)
    yield _AFTER_OTHERS
    h_slot[...] = h2
    hn_buf[...] = hn2


STAGE_A_COST = (
    (0, 800),
    (1024, 500),
    (1024, 1000),
    (500, 600),
    (1024, 700),
    (512, 1800),
    (0, 800),
    (1024, 1100),
    (2048, 900),
    (2048, 1000),
    (512, 700),
    (512, 700),
    (2048, 1700),
)


def _stage_b(h_buf, hn_buf, p, act_scr, o_ref):
    for c in range(D_FF // FF_CHUNK):
        yield
        sl = slice(c * FF_CHUNK, (c + 1) * FF_CHUNK)
        g = _dot(hn_buf[...], p["w_gate"][:, sl])
        up = _dot(hn_buf[...], p["w_up"][:, sl])
        act_scr[:, sl] = (g * _sigmoid(g) * up).astype(BF16)
    rows = act_scr.shape[0] // DOWN_ROW_BLOCKS
    cols = D_MODEL // DOWN_COL_BLOCKS
    for r in range(DOWN_ROW_BLOCKS):
        rs = slice(r * rows, (r + 1) * rows)
        ys = []
        for c in range(DOWN_COL_BLOCKS):
            yield
            ys.append(_dot(act_scr[rs, :], p["w_down"][:, c * cols:(c + 1) * cols]))
        y = jnp.concatenate(ys, axis=1)
        o_ref[rs, :] = h_buf[rs, :] + _rms(y, p["n_post_ffn"][...])


_DOWN_PIECE = (5632 // (DOWN_ROW_BLOCKS * DOWN_COL_BLOCKS), 900 // (DOWN_ROW_BLOCKS * DOWN_COL_BLOCKS))
STAGE_B_COST = (((1024, 250),) * (D_FF // FF_CHUNK)
                + (_DOWN_PIECE,) * (DOWN_ROW_BLOCKS * DOWN_COL_BLOCKS))

_AFTER_OTHERS = object()


def _interleave(lead, lead_cost, fill, fill_cost):
    fill_total = sum(m for m, _ in fill_cost)
    valu_total = sum(v for _, v in lead_cost)
    next(lead)
    next(fill)
    fill_i, fill_done, valu_done, marker = 0, 0.0, 0.0, None
    for k in range(len(lead_cost)):
        marker = next(lead)
        valu_done += lead_cost[k][1]
        target = fill_total * valu_done / valu_total
        while fill_i < len(fill_cost) and fill_done + fill_cost[fill_i][0] / 2 <= target:
            next(fill, None)
            fill_done += fill_cost[fill_i][0]
            fill_i += 1
    if marker is not _AFTER_OTHERS:
        raise ValueError("lead_cost does not match the lead generator's phases")
    for _ in range(fill_i, len(fill_cost)):
        next(fill, None)
    leftovers = [next(fill, None), next(lead, None)]
    if leftovers != [None, None]:
        raise ValueError("phase generators yielded more phases than their cost tables list")


_PARAM_ORDER = (
    "n_pre_mix", "w_in", "ln_v_g", "ln_v_b", "w_s", "b_s", "conv_w", "conv_b", "w_gate_lru",
    "b_a", "b_x", "lam", "g_out_gmlp", "g_out_lru", "w_out", "n_post_mix",
    "n_pre_x", "w_q", "kt", "v", "w_o", "n_post_x",
    "n_pre_ffn", "w_gate", "w_up", "w_down", "n_post_ffn",
)


def _layer_kernel(tiles_per_seq, x_ref, *rest):
    n_p = len(_PARAM_ORDER)
    p = dict(zip(_PARAM_ORDER, rest[:n_p]))
    o_ref = rest[n_p]
    h_buf, hn_buf, act_scr, h_carry, x_tail = rest[n_p + 1:]
    step = pl.program_id(0)
    n_tiles = pl.num_programs(0) - (PIPELINE_DEPTH - 1)
    tile = jnp.minimum(step, n_tiles - 1)

    @pl.when(step == 0)
    def _():
        h_buf[...] = jnp.zeros_like(h_buf)
        hn_buf[...] = jnp.zeros_like(hn_buf)

    @pl.when(lax.rem(tile, tiles_per_seq) == 0)
    def _():
        x_tail[...] = jnp.zeros_like(x_tail)
        h_carry[...] = jnp.zeros_like(h_carry)

    stage_a = _stage_a(x_ref, p, h_buf, hn_buf, h_carry, x_tail)
    stage_b = _stage_b(h_buf, hn_buf, p, act_scr, o_ref)
    _interleave(stage_a, STAGE_A_COST, stage_b, STAGE_B_COST)


def _resident(shape):
    zeros = (0,) * len(shape)
    return pl.BlockSpec(shape, lambda *_: zeros, pipeline_mode=pl.Buffered(1))


def _block_diag(w):
    h, d, _ = w.shape
    eye = jnp.eye(h, dtype=w.dtype)
    return jnp.einsum('hij,hk->hikj', w, eye).reshape(h * d, h * d)


def _row(p):
    return p.reshape(1, -1).astype(F32)


def kernel(x, mem, w_in, ln_v_g, ln_v_b, w_s, b_s, conv_w, conv_b, w_a, b_a, w_x, b_x, lam,
           g_out_gmlp, g_out_lru, w_out, w_q, w_kv, w_o, w_gate, w_up, w_down, n_pre_mix,
           n_post_mix, n_pre_x, n_mem, n_post_x, n_pre_ffn, n_post_ffn):
    batch, seq, _ = x.shape
    depth = w_in.shape[0]
    t = TOKENS
    assert seq % t == 0 and t % CHUNK == 0
    tiles_per_seq = seq // t
    n_tiles = batch * tiles_per_seq
    heads_per_block = GATE_BLOCK // R_HEAD_DIM

    def tile_of_a(i):
        return jnp.minimum(i, n_tiles - 1)

    def tile_of_b(i):
        return jnp.maximum(i - (PIPELINE_DEPTH - 1), 0)

    x_spec = pl.BlockSpec((None, t, D_MODEL),
                          lambda i: (tile_of_a(i) // tiles_per_seq, tile_of_a(i) % tiles_per_seq, 0))
    o_spec = pl.BlockSpec((None, t, D_MODEL),
                          lambda i: (tile_of_b(i) // tiles_per_seq, tile_of_b(i) % tiles_per_seq, 0))
    kt_spec = pl.BlockSpec((None, D_MODEL, N_MEM), lambda i: (tile_of_a(i) // tiles_per_seq, 0, 0))
    v_spec = pl.BlockSpec((None, N_MEM, D_MODEL), lambda i: (tile_of_a(i) // tiles_per_seq, 0, 0))

    h = x
    for l in range(depth):
        kv_in = [mem, _row(n_mem[l]), w_kv[l].astype(BF16)]
        kt, v = pl.pallas_call(
            _kv_kernel,
            grid=(batch,),
            in_specs=[pl.BlockSpec((None, N_MEM, D_MODEL), lambda b: (b, 0, 0)),
                      _resident(kv_in[1].shape), _resident(kv_in[2].shape)],
            out_specs=[pl.BlockSpec((None, D_MODEL, N_MEM), lambda b: (b, 0, 0)),
                       pl.BlockSpec((None, N_MEM, D_MODEL), lambda b: (b, 0, 0))],
            out_shape=[jax.ShapeDtypeStruct((batch, D_MODEL, N_MEM), BF16),
                       jax.ShapeDtypeStruct((batch, N_MEM, D_MODEL), BF16)],
            compiler_params=pltpu.CompilerParams(dimension_semantics=("arbitrary",),
                                                 vmem_limit_bytes=VMEM_LIMIT_BYTES),
            name="kv_proj",
        )(*kv_in)

        gate_blocks = []
        for blk in range(R_WIDTH // GATE_BLOCK):
            hs = slice(blk * heads_per_block, (blk + 1) * heads_per_block)
            gate_blocks.append(jnp.concatenate(
                [_block_diag(w_a[l, hs]), _block_diag(w_x[l, hs])], axis=1))

        params = {
            "n_pre_mix": _row(n_pre_mix[l]),
            "w_in": w_in[l].astype(BF16),
            "ln_v_g": _row(ln_v_g[l]),
            "ln_v_b": _row(ln_v_b[l]),
            "w_s": w_s[l].astype(BF16),
            "b_s": jnp.repeat(b_s[l].T, G_HEAD_DIM, axis=1).astype(F32),
            "conv_w": conv_w[l].astype(F32),
            "conv_b": _row(conv_b[l]),
            "w_gate_lru": jnp.stack(gate_blocks).astype(BF16),
            "b_a": _row(b_a[l]),
            "b_x": _row(b_x[l]),
            "lam": _row(lam[l]),
            "g_out_gmlp": _row(g_out_gmlp[l]),
            "g_out_lru": _row(g_out_lru[l]),
            "w_out": w_out[l].astype(BF16),
            "n_post_mix": _row(n_post_mix[l]),
            "n_pre_x": _row(n_pre_x[l]),
            "w_q": w_q[l].astype(BF16),
            "kt": kt,
            "v": v,
            "w_o": w_o[l].astype(BF16),
            "n_post_x": _row(n_post_x[l]),
            "n_pre_ffn": _row(n_pre_ffn[l]),
            "w_gate": w_gate[l].astype(BF16),
            "w_up": w_up[l].astype(BF16),
            "w_down": w_down[l].astype(BF16),
            "n_post_ffn": _row(n_post_ffn[l]),
        }
        specs = {"kt": kt_spec, "v": v_spec}
        in_specs = [x_spec] + [specs.get(k) or _resident(params[k].shape) for k in _PARAM_ORDER]

        def layer_body(*refs):
            _layer_kernel(tiles_per_seq, *refs)

        h = pl.pallas_call(
            layer_body,
            grid=(n_tiles + PIPELINE_DEPTH - 1,),
            in_specs=in_specs,
            out_specs=o_spec,
            out_shape=jax.ShapeDtypeStruct((batch, seq, D_MODEL), F32),
            scratch_shapes=[
                pltpu.VMEM((t, D_MODEL), F32),
                pltpu.VMEM((t, D_MODEL), BF16),
                pltpu.VMEM((t, D_FF), BF16),
                pltpu.VMEM((1, R_WIDTH), F32),
                pltpu.VMEM((CONV_W - 1, SUBLANES, R_WIDTH), F32),
            ],
            compiler_params=pltpu.CompilerParams(dimension_semantics=("arbitrary",),
                                                 vmem_limit_bytes=VMEM_LIMIT_BYTES),
            name="layer",
        )(h, *[params[k] for k in _PARAM_ORDER])
    return h
```

```python
import math

import jax
import jax.numpy as jnp
from jax import lax
from jax.experimental import pallas as pl
from jax.experimental.pallas import tpu as pltpu

D_MODEL = 1024
G_WIDTH = 512
R_WIDTH = 512
G_HEADS = 4
G_HEAD_DIM = 128
CHUNK = 128
R_HEADS = 8
R_HEAD_DIM = 64
CONV_W = 4
RG_C = 8.0
N_MEM = 256
X_HEADS = 4
X_HEAD_DIM = 256
D_FF = 2816
EPS = 1e-6

SUBLANES = 8
NEG_LOG2E = -math.log2(math.e)
GATE_BLOCK = 256
TOKENS = 512
FF_CHUNK = 256
DOWN_ROW_BLOCKS = 2
DOWN_COL_BLOCKS = 2
PIPELINE_DEPTH = 2
VMEM_LIMIT_BYTES = 60 * 1024 * 1024

BF16 = jnp.bfloat16
F32 = jnp.float32


def _rms(x, g):
    return x * lax.rsqrt(jnp.mean(x * x, axis=-1, keepdims=True) + EPS) * g


def _sigmoid(x):
    return 1.0 / (1.0 + jnp.exp2(x * NEG_LOG2E))


def _gelu(x):
    k1 = 2.0 * math.sqrt(2.0 / math.pi) * NEG_LOG2E
    return x / (1.0 + jnp.exp2(x * (k1 + (k1 * 0.044715) * (x * x))))


def _dot(a, b):
    return jnp.dot(a, b, preferred_element_type=F32)


def _kv_kernel(mem_ref, n_mem_ref, w_kv_ref, kt_ref, v_ref):
    m = _rms(mem_ref[...], n_mem_ref[...]).astype(BF16)
    width = X_HEADS * X_HEAD_DIM
    k = _dot(m, w_kv_ref[:, :width])
    v = _dot(m, w_kv_ref[:, width:])
    kt_ref[...] = k.T.astype(BF16)
    v_ref[...] = v.astype(BF16)


def _linear_scan(a, b, h0, a_scr, b_scr, h_scr):
    t, c = a.shape
    groups = t // SUBLANES
    a3 = a.reshape(groups, SUBLANES, c)
    b3 = b.reshape(groups, SUBLANES, c)
    row = lax.broadcasted_iota(jnp.int32, a3.shape, 1)
    shift = 1
    while shift < SUBLANES:
        keep = row >= shift
        a_prev = jnp.where(keep, pltpu.roll(a3, shift, 1), 1.0)
        b_prev = jnp.where(keep, pltpu.roll(b3, shift, 1), 0.0)
        b3 = a3 * b_prev + b3
        a3 = a3 * a_prev
        shift *= 2
    a_scr[...] = a3
    b_scr[...] = b3
    carry = h0
    for g in range(groups):
        blk = a_scr[g] * carry + b_scr[g]
        h_scr[g] = blk
        carry = blk[SUBLANES - 1:SUBLANES, :]
    return h_scr[...].reshape(t, c), carry


def _exchange(blocks):
    sub = lax.broadcasted_iota(jnp.int32, blocks[0].shape, 0)
    v = list(blocks)
    d = SUBLANES // 2
    while d >= 1:
        low = (sub & d) == 0
        nxt = list(v)
        for k in range(SUBLANES):
            if k & d == 0:
                nxt[k] = jnp.where(low, v[k], pltpu.roll(v[k + d], d, 0))
                nxt[k + d] = jnp.where(low, pltpu.roll(v[k], SUBLANES - d, 0), v[k + d])
        v = nxt
        d //= 2
    return v


def _to_segment_layout(x):
    seg_len = x.shape[0] // SUBLANES
    out = []
    for m in range(seg_len // SUBLANES):
        rows = [x[k * seg_len + m * SUBLANES:k * seg_len + (m + 1) * SUBLANES, :]
                for k in range(SUBLANES)]
        out.extend(_exchange(rows))
    return out


def _from_segment_layout(xs):
    seg_len = len(xs)
    groups = [_exchange(xs[m * SUBLANES:(m + 1) * SUBLANES]) for m in range(seg_len // SUBLANES)]
    return jnp.concatenate([groups[m][k] for k in range(SUBLANES)
                            for m in range(seg_len // SUBLANES)], axis=0)


def _stage_a(x_ref, p, h_slot, hn_buf, h_carry, xr_scr, a_scr, b_scr, h_scr):
    yield
    x = x_ref[...]
    t = x.shape[0]
    w_in_ref = p["w_in"]
    hn = _rms(x, p["n_pre_mix"][...]).astype(BF16)

    yield
    u = _gelu(_dot(hn, w_in_ref[:, 0:G_WIDTH]))
    yield
    gv = _gelu(_dot(hn, w_in_ref[:, G_WIDTH:2 * G_WIDTH]))
    mu = jnp.mean(gv, axis=-1, keepdims=True)
    cen = gv - mu
    var = jnp.mean(cen * cen, axis=-1, keepdims=True)
    v = (cen * lax.rsqrt(var + EPS) * p["ln_v_g"][...] + p["ln_v_b"][...]).astype(BF16)
    yield
    tri_r = lax.broadcasted_iota(jnp.int32, (CHUNK, CHUNK), 0)
    tri_c = lax.broadcasted_iota(jnp.int32, (CHUNK, CHUNK), 1)
    causal = tri_r >= tri_c
    ws = [jnp.where(causal, p["w_s"][hd], jnp.zeros((CHUNK, CHUNK), BF16)) for hd in range(G_HEADS)]
    bs = p["b_s"][...]
    rows = []
    for c in range(t // CHUNK):
        cols = []
        for hd in range(G_HEADS):
            vb = v[c * CHUNK:(c + 1) * CHUNK, hd * G_HEAD_DIM:(hd + 1) * G_HEAD_DIM]
            cols.append(_dot(ws[hd], vb))
        rows.append(jnp.concatenate(cols, axis=1) + bs)
    s = jnp.concatenate(rows, axis=0)
    y_g = _rms(u * s, p["g_out_gmlp"][...]).astype(BF16)

    yield
    xr = _dot(hn, w_in_ref[:, 2 * G_WIDTH:2 * G_WIDTH + R_WIDTH])
    xr_scr[SUBLANES:SUBLANES + t, :] = xr
    cw = p["conv_w"][...]
    xc = p["conv_b"][...]
    for k in range(CONV_W):
        off = SUBLANES - (CONV_W - 1) + k
        xc = xc + xr_scr[off:off + t, :] * cw[k:k + 1, :]
    xr_scr[0:SUBLANES, :] = xr[t - SUBLANES:t, :]
    xc_b = xc.astype(BF16)
    yield
    ga, gx = [], []
    for blk in range(R_WIDTH // GATE_BLOCK):
        g2 = _dot(xc_b[:, blk * GATE_BLOCK:(blk + 1) * GATE_BLOCK], p["w_gate_lru"][blk])
        ga.append(g2[:, :GATE_BLOCK])
        gx.append(g2[:, GATE_BLOCK:])
    r = _sigmoid(jnp.concatenate(ga, axis=1) + p["b_a"][...])
    i = _sigmoid(jnp.concatenate(gx, axis=1) + p["b_x"][...])
    nl = -p["lam"][...]
    softplus = jnp.maximum(nl, 0.0) + jnp.log1p(jnp.exp(-jnp.abs(nl)))
    neg_log_a = r * (RG_C * softplus)
    a = jnp.exp2(neg_log_a * NEG_LOG2E)
    mult = jnp.sqrt(jnp.tanh(neg_log_a) * (a * a + 1.0))
    bterm = mult * (i * xc)
    yield
    hseq, h_last = _linear_scan(a, bterm, h_carry[...], a_scr, b_scr, h_scr)
    h_carry[...] = h_last
    yield
    gr = _dot(hn, w_in_ref[:, 2 * G_WIDTH + R_WIDTH:])
    y_r = _rms(_gelu(gr) * hseq, p["g_out_lru"][...]).astype(BF16)
    yield
    y = _dot(jnp.concatenate([y_g, y_r], axis=1), p["w_out"][...])
    h1 = x + _rms(y, p["n_post_mix"][...])

    yield
    hn = _rms(h1, p["n_pre_x"][...]).astype(BF16)
    q = _dot(hn, p["w_q"][...]).astype(BF16)
    outs = []
    for hd in range(X_HEADS):
        if hd % 2 == 0:
            yield
        sl = slice(hd * X_HEAD_DIM, (hd + 1) * X_HEAD_DIM)
        sc = _dot(q[:, sl], p["kt"][sl, :]) * (X_HEAD_DIM ** -0.5)
        m = jnp.max(sc, axis=-1, keepdims=True)
        e = jnp.exp(sc - m)
        pr = (e / jnp.sum(e, axis=-1, keepdims=True)).astype(BF16)
        outs.append(_dot(pr, p["v"][:, sl]).astype(BF16))
    yield
    y = _dot(jnp.concatenate(outs, axis=1), p["w_o"][...])
    h2 = h1 + _rms(y, p["n_post_x"][...])
    hn2 = _rms(h2, p["n_pre_ffn"][...]).astype(BF16)
    yield _AFTER_OTHERS
    h_slot[...] = h2
    hn_buf[...] = hn2


STAGE_A_COST = (
    (0, 800),
    (1024, 500),
    (1024, 1000),
    (500, 600),
    (1024, 700),
    (512, 1800),
    (0, 800),
    (1024, 1100),
    (2048, 900),
    (2048, 1000),
    (512, 700),
    (512, 700),
    (2048, 1700),
)


def _stage_b(h_buf, hn_buf, p, act_scr, o_ref):
    for c in range(D_FF // FF_CHUNK):
        yield
        sl = slice(c * FF_CHUNK, (c + 1) * FF_CHUNK)
        g = _dot(hn_buf[...], p["w_gate"][:, sl])
        up = _dot(hn_buf[...], p["w_up"][:, sl])
        act_scr[:, sl] = (g * _sigmoid(g) * up).astype(BF16)
    rows = act_scr.shape[0] // DOWN_ROW_BLOCKS
    cols = D_MODEL // DOWN_COL_BLOCKS
    for r in range(DOWN_ROW_BLOCKS):
        rs = slice(r * rows, (r + 1) * rows)
        ys = []
        for c in range(DOWN_COL_BLOCKS):
            yield
            ys.append(_dot(act_scr[rs, :], p["w_down"][:, c * cols:(c + 1) * cols]))
        y = jnp.concatenate(ys, axis=1)
        o_ref[rs, :] = h_buf[rs, :] + _rms(y, p["n_post_ffn"][...])


_DOWN_PIECE = (5632 // (DOWN_ROW_BLOCKS * DOWN_COL_BLOCKS), 900 // (DOWN_ROW_BLOCKS * DOWN_COL_BLOCKS))
STAGE_B_COST = (((1024, 250),) * (D_FF // FF_CHUNK)
                + (_DOWN_PIECE,) * (DOWN_ROW_BLOCKS * DOWN_COL_BLOCKS))

_AFTER_OTHERS = object()


def _interleave(lead, lead_cost, fill, fill_cost):
    fill_total = sum(m for m, _ in fill_cost)
    valu_total = sum(v for _, v in lead_cost)
    next(lead)
    next(fill)
    fill_i, fill_done, valu_done, marker = 0, 0.0, 0.0, None
    for k in range(len(lead_cost)):
        marker = next(lead)
        valu_done += lead_cost[k][1]
        target = fill_total * valu_done / valu_total
        while fill_i < len(fill_cost) and fill_done + fill_cost[fill_i][0] / 2 <= target:
            next(fill, None)
            fill_done += fill_cost[fill_i][0]
            fill_i += 1
    if marker is not _AFTER_OTHERS:
        raise ValueError("lead_cost does not match the lead generator's phases")
    for _ in range(fill_i, len(fill_cost)):
        next(fill, None)
    leftovers = [next(fill, None), next(lead, None)]
    if leftovers != [None, None]:
        raise ValueError("phase generators yielded more phases than their cost tables list")


_PARAM_ORDER = (
    "n_pre_mix", "w_in", "ln_v_g", "ln_v_b", "w_s", "b_s", "conv_w", "conv_b", "w_gate_lru",
    "b_a", "b_x", "lam", "g_out_gmlp", "g_out_lru", "w_out", "n_post_mix",
    "n_pre_x", "w_q", "kt", "v", "w_o", "n_post_x",
    "n_pre_ffn", "w_gate", "w_up", "w_down", "n_post_ffn",
)


def _layer_kernel(tiles_per_seq, x_ref, *rest):
    n_p = len(_PARAM_ORDER)
    p = dict(zip(_PARAM_ORDER, rest[:n_p]))
    o_ref = rest[n_p]
    h_buf, hn_buf, act_scr, h_carry, xr_scr, a_scr, b_scr, h_scr = rest[n_p + 1:]
    step = pl.program_id(0)
    n_tiles = pl.num_programs(0) - (PIPELINE_DEPTH - 1)
    tile = jnp.minimum(step, n_tiles - 1)

    @pl.when(step == 0)
    def _():
        h_buf[...] = jnp.zeros_like(h_buf)
        hn_buf[...] = jnp.zeros_like(hn_buf)

    @pl.when(lax.rem(tile, tiles_per_seq) == 0)
    def _():
        xr_scr[0:SUBLANES, :] = jnp.zeros((SUBLANES, R_WIDTH), F32)
        h_carry[...] = jnp.zeros_like(h_carry)

    stage_a = _stage_a(x_ref, p, h_buf, hn_buf, h_carry, xr_scr, a_scr, b_scr, h_scr)
    stage_b = _stage_b(h_buf, hn_buf, p, act_scr, o_ref)
    _interleave(stage_a, STAGE_A_COST, stage_b, STAGE_B_COST)


def _resident(shape):
    zeros = (0,) * len(shape)
    return pl.BlockSpec(shape, lambda *_: zeros, pipeline_mode=pl.Buffered(1))


def _block_diag(w):
    h, d, _ = w.shape
    eye = jnp.eye(h, dtype=w.dtype)
    return jnp.einsum('hij,hk->hikj', w, eye).reshape(h * d, h * d)


def _row(p):
    return p.reshape(1, -1).astype(F32)


def kernel(x, mem, w_in, ln_v_g, ln_v_b, w_s, b_s, conv_w, conv_b, w_a, b_a, w_x, b_x, lam,
           g_out_gmlp, g_out_lru, w_out, w_q, w_kv, w_o, w_gate, w_up, w_down, n_pre_mix,
           n_post_mix, n_pre_x, n_mem, n_post_x, n_pre_ffn, n_post_ffn):
    batch, seq, _ = x.shape
    depth = w_in.shape[0]
    t = TOKENS
    assert seq % t == 0 and t % CHUNK == 0
    tiles_per_seq = seq // t
    n_tiles = batch * tiles_per_seq
    heads_per_block = GATE_BLOCK // R_HEAD_DIM

    def tile_of_a(i):
        return jnp.minimum(i, n_tiles - 1)

    def tile_of_b(i):
        return jnp.maximum(i - (PIPELINE_DEPTH - 1), 0)

    x_spec = pl.BlockSpec((None, t, D_MODEL),
                          lambda i: (tile_of_a(i) // tiles_per_seq, tile_of_a(i) % tiles_per_seq, 0))
    o_spec = pl.BlockSpec((None, t, D_MODEL),
                          lambda i: (tile_of_b(i) // tiles_per_seq, tile_of_b(i) % tiles_per_seq, 0))
    kt_spec = pl.BlockSpec((None, D_MODEL, N_MEM), lambda i: (tile_of_a(i) // tiles_per_seq, 0, 0))
    v_spec = pl.BlockSpec((None, N_MEM, D_MODEL), lambda i: (tile_of_a(i) // tiles_per_seq, 0, 0))

    h = x
    for l in range(depth):
        kv_in = [mem, _row(n_mem[l]), w_kv[l].astype(BF16)]
        kt, v = pl.pallas_call(
            _kv_kernel,
            grid=(batch,),
            in_specs=[pl.BlockSpec((None, N_MEM, D_MODEL), lambda b: (b, 0, 0)),
                      _resident(kv_in[1].shape), _resident(kv_in[2].shape)],
            out_specs=[pl.BlockSpec((None, D_MODEL, N_MEM), lambda b: (b, 0, 0)),
                       pl.BlockSpec((None, N_MEM, D_MODEL), lambda b: (b, 0, 0))],
            out_shape=[jax.ShapeDtypeStruct((batch, D_MODEL, N_MEM), BF16),
                       jax.ShapeDtypeStruct((batch, N_MEM, D_MODEL), BF16)],
            compiler_params=pltpu.CompilerParams(dimension_semantics=("arbitrary",),
                                                 vmem_limit_bytes=VMEM_LIMIT_BYTES),
            name="kv_proj",
        )(*kv_in)

        gate_blocks = []
        for blk in range(R_WIDTH // GATE_BLOCK):
            hs = slice(blk * heads_per_block, (blk + 1) * heads_per_block)
            gate_blocks.append(jnp.concatenate(
                [_block_diag(w_a[l, hs]), _block_diag(w_x[l, hs])], axis=1))

        params = {
            "n_pre_mix": _row(n_pre_mix[l]),
            "w_in": w_in[l].astype(BF16),
            "ln_v_g": _row(ln_v_g[l]),
            "ln_v_b": _row(ln_v_b[l]),
            "w_s": w_s[l].astype(BF16),
            "b_s": jnp.repeat(b_s[l].T, G_HEAD_DIM, axis=1).astype(F32),
            "conv_w": conv_w[l].astype(F32),
            "conv_b": _row(conv_b[l]),
            "w_gate_lru": jnp.stack(gate_blocks).astype(BF16),
            "b_a": _row(b_a[l]),
            "b_x": _row(b_x[l]),
            "lam": _row(lam[l]),
            "g_out_gmlp": _row(g_out_gmlp[l]),
            "g_out_lru": _row(g_out_lru[l]),
            "w_out": w_out[l].astype(BF16),
            "n_post_mix": _row(n_post_mix[l]),
            "n_pre_x": _row(n_pre_x[l]),
            "w_q": w_q[l].astype(BF16),
            "kt": kt,
            "v": v,
            "w_o": w_o[l].astype(BF16),
            "n_post_x": _row(n_post_x[l]),
            "n_pre_ffn": _row(n_pre_ffn[l]),
            "w_gate": w_gate[l].astype(BF16),
            "w_up": w_up[l].astype(BF16),
            "w_down": w_down[l].astype(BF16),
            "n_post_ffn": _row(n_post_ffn[l]),
        }
        specs = {"kt": kt_spec, "v": v_spec}
        in_specs = [x_spec] + [specs.get(k) or _resident(params[k].shape) for k in _PARAM_ORDER]

        def layer_body(*refs):
            _layer_kernel(tiles_per_seq, *refs)

        h = pl.pallas_call(
            layer_body,
            grid=(n_tiles + PIPELINE_DEPTH - 1,),
            in_specs=in_specs,
            out_specs=o_spec,
            out_shape=jax.ShapeDtypeStruct((batch, seq, D_MODEL), F32),
            scratch_shapes=[
                pltpu.VMEM((t, D_MODEL), F32),
                pltpu.VMEM((t, D_MODEL), BF16),
                pltpu.VMEM((t, D_FF), BF16),
                pltpu.VMEM((1, R_WIDTH), F32),
                pltpu.VMEM((t + SUBLANES, R_WIDTH), F32),
                pltpu.VMEM((t // SUBLANES, SUBLANES, R_WIDTH), F32),
                pltpu.VMEM((t // SUBLANES, SUBLANES, R_WIDTH), F32),
                pltpu.VMEM((t // SUBLANES, SUBLANES, R_WIDTH), F32),
            ],
            compiler_params=pltpu.CompilerParams(dimension_semantics=("arbitrary",),
                                                 vmem_limit_bytes=VMEM_LIMIT_BYTES),
            name="layer",
        )(h, *[params[k] for k in _PARAM_ORDER])
    return h
```

```python
import math

import jax
import jax.numpy as jnp
from jax import lax
from jax.experimental import pallas as pl
from jax.experimental.pallas import tpu as pltpu

D_MODEL = 1024
G_WIDTH = 512
R_WIDTH = 512
G_HEADS = 4
G_HEAD_DIM = 128
CHUNK = 128
R_HEADS = 8
R_HEAD_DIM = 64
CONV_W = 4
RG_C = 8.0
N_MEM = 256
X_HEADS = 4
X_HEAD_DIM = 256
D_FF = 2816
EPS = 1e-6

SUBLANES = 8
NEG_LOG2E = -math.log2(math.e)
GATE_BLOCK = 256
TOKENS = 256
FF_CHUNK = 256
DOWN_ROW_BLOCKS = 1
DOWN_COL_BLOCKS = 2
PIPELINE_DEPTH = 2
VMEM_LIMIT_BYTES = 60 * 1024 * 1024

BF16 = jnp.bfloat16
F32 = jnp.float32


def _rms(x, g):
    return x * lax.rsqrt(jnp.mean(x * x, axis=-1, keepdims=True) + EPS) * g


def _sigmoid(x):
    return 1.0 / (1.0 + jnp.exp2(x * NEG_LOG2E))


def _gelu(x):
    k1 = 2.0 * math.sqrt(2.0 / math.pi) * NEG_LOG2E
    return x / (1.0 + jnp.exp2(x * (k1 + (k1 * 0.044715) * (x * x))))


def _dot(a, b):
    return jnp.dot(a, b, preferred_element_type=F32)


def _kv_kernel(mem_ref, n_mem_ref, w_kv_ref, kt_ref, v_ref):
    m = _rms(mem_ref[...], n_mem_ref[...]).astype(BF16)
    width = X_HEADS * X_HEAD_DIM
    k = _dot(m, w_kv_ref[:, :width])
    v = _dot(m, w_kv_ref[:, width:])
    kt_ref[...] = k.T.astype(BF16)
    v_ref[...] = v.astype(BF16)


def _linear_scan(a, b, h0, a_scr, b_scr, h_scr):
    t, c = a.shape
    groups = t // SUBLANES
    a3 = a.reshape(groups, SUBLANES, c)
    b3 = b.reshape(groups, SUBLANES, c)
    row = lax.broadcasted_iota(jnp.int32, a3.shape, 1)
    shift = 1
    while shift < SUBLANES:
        keep = row >= shift
        a_prev = jnp.where(keep, pltpu.roll(a3, shift, 1), 1.0)
        b_prev = jnp.where(keep, pltpu.roll(b3, shift, 1), 0.0)
        b3 = a3 * b_prev + b3
        a3 = a3 * a_prev
        shift *= 2
    a_scr[...] = a3
    b_scr[...] = b3
    carry = h0
    for g in range(groups):
        blk = a_scr[g] * carry + b_scr[g]
        h_scr[g] = blk
        carry = blk[SUBLANES - 1:SUBLANES, :]
    return h_scr[...].reshape(t, c), carry


def _exchange(blocks):
    sub = lax.broadcasted_iota(jnp.int32, blocks[0].shape, 0)
    v = list(blocks)
    d = SUBLANES // 2
    while d >= 1:
        low = (sub & d) == 0
        nxt = list(v)
        for k in range(SUBLANES):
            if k & d == 0:
                nxt[k] = jnp.where(low, v[k], pltpu.roll(v[k + d], d, 0))
                nxt[k + d] = jnp.where(low, pltpu.roll(v[k], SUBLANES - d, 0), v[k + d])
        v = nxt
        d //= 2
    return v


def _to_segment_layout(x):
    seg_len = x.shape[0] // SUBLANES
    out = []
    for m in range(seg_len // SUBLANES):
        rows = [x[k * seg_len + m * SUBLANES:k * seg_len + (m + 1) * SUBLANES, :]
                for k in range(SUBLANES)]
        out.extend(_exchange(rows))
    return out


def _from_segment_layout(xs):
    seg_len = len(xs)
    groups = [_exchange(xs[m * SUBLANES:(m + 1) * SUBLANES]) for m in range(seg_len // SUBLANES)]
    return jnp.concatenate([groups[m][k] for k in range(SUBLANES)
                            for m in range(seg_len // SUBLANES)], axis=0)


def _stage_a(x_ref, p, h_slot, hn_buf, h_carry, xr_scr, a_scr, b_scr, h_scr):
    yield
    x = x_ref[...]
    t = x.shape[0]
    w_in_ref = p["w_in"]
    hn = _rms(x, p["n_pre_mix"][...]).astype(BF16)

    yield
    u = _gelu(_dot(hn, w_in_ref[:, 0:G_WIDTH]))
    yield
    gv = _gelu(_dot(hn, w_in_ref[:, G_WIDTH:2 * G_WIDTH]))
    mu = jnp.mean(gv, axis=-1, keepdims=True)
    cen = gv - mu
    var = jnp.mean(cen * cen, axis=-1, keepdims=True)
    v = (cen * lax.rsqrt(var + EPS) * p["ln_v_g"][...] + p["ln_v_b"][...]).astype(BF16)
    yield
    tri_r = lax.broadcasted_iota(jnp.int32, (CHUNK, CHUNK), 0)
    tri_c = lax.broadcasted_iota(jnp.int32, (CHUNK, CHUNK), 1)
    causal = tri_r >= tri_c
    ws = [jnp.where(causal, p["w_s"][hd], jnp.zeros((CHUNK, CHUNK), BF16)) for hd in range(G_HEADS)]
    bs = p["b_s"][...]
    rows = []
    for c in range(t // CHUNK):
        cols = []
        for hd in range(G_HEADS):
            vb = v[c * CHUNK:(c + 1) * CHUNK, hd * G_HEAD_DIM:(hd + 1) * G_HEAD_DIM]
            cols.append(_dot(ws[hd], vb))
        rows.append(jnp.concatenate(cols, axis=1) + bs)
    s = jnp.concatenate(rows, axis=0)
    y_g = _rms(u * s, p["g_out_gmlp"][...]).astype(BF16)

    yield
    xr = _dot(hn, w_in_ref[:, 2 * G_WIDTH:2 * G_WIDTH + R_WIDTH])
    xr_scr[SUBLANES:SUBLANES + t, :] = xr
    cw = p["conv_w"][...]
    xc = p["conv_b"][...]
    for k in range(CONV_W):
        off = SUBLANES - (CONV_W - 1) + k
        xc = xc + xr_scr[off:off + t, :] * cw[k:k + 1, :]
    xr_scr[0:SUBLANES, :] = xr[t - SUBLANES:t, :]
    xc_b = xc.astype(BF16)
    yield
    ga, gx = [], []
    for blk in range(R_WIDTH // GATE_BLOCK):
        g2 = _dot(xc_b[:, blk * GATE_BLOCK:(blk + 1) * GATE_BLOCK], p["w_gate_lru"][blk])
        ga.append(g2[:, :GATE_BLOCK])
        gx.append(g2[:, GATE_BLOCK:])
    r = _sigmoid(jnp.concatenate(ga, axis=1) + p["b_a"][...])
    i = _sigmoid(jnp.concatenate(gx, axis=1) + p["b_x"][...])
    nl = -p["lam"][...]
    softplus = jnp.maximum(nl, 0.0) + jnp.log1p(jnp.exp(-jnp.abs(nl)))
    neg_log_a = r * (RG_C * softplus)
    a = jnp.exp2(neg_log_a * NEG_LOG2E)
    mult = jnp.sqrt(jnp.tanh(neg_log_a) * (a * a + 1.0))
    bterm = mult * (i * xc)
    yield
    hseq, h_last = _linear_scan(a, bterm, h_carry[...], a_scr, b_scr, h_scr)
    h_carry[...] = h_last
    yield
    gr = _dot(hn, w_in_ref[:, 2 * G_WIDTH + R_WIDTH:])
    y_r = _rms(_gelu(gr) * hseq, p["g_out_lru"][...]).astype(BF16)
    yield
    y = _dot(jnp.concatenate([y_g, y_r], axis=1), p["w_out"][...])
    h1 = x + _rms(y, p["n_post_mix"][...])

    yield
    hn = _rms(h1, p["n_pre_x"][...]).astype(BF16)
    q = _dot(hn, p["w_q"][...]).astype(BF16)
    outs = []
    for hd in range(X_HEADS):
        if hd % 2 == 0:
            yield
        sl = slice(hd * X_HEAD_DIM, (hd + 1) * X_HEAD_DIM)
        sc = _dot(q[:, sl], p["kt"][sl, :]) * (X_HEAD_DIM ** -0.5)
        m = jnp.max(sc, axis=-1, keepdims=True)
        e = jnp.exp(sc - m)
        pr = (e / jnp.sum(e, axis=-1, keepdims=True)).astype(BF16)
        outs.append(_dot(pr, p["v"][:, sl]).astype(BF16))
    yield
    y = _dot(jnp.concatenate(outs, axis=1), p["w_o"][...])
    h2 = h1 + _rms(y, p["n_post_x"][...])
    hn2 = _rms(h2, p["n_pre_ffn"][...]).astype(BF16)
    yield _AFTER_OTHERS
    h_slot[...] = h2
    hn_buf[...] = hn2


STAGE_A_COST = (
    (0, 800),
    (1024, 500),
    (1024, 1000),
    (500, 600),
    (1024, 700),
    (512, 1800),
    (0, 800),
    (1024, 1100),
    (2048, 900),
    (2048, 1000),
    (512, 700),
    (512, 700),
    (2048, 1700),
)


def _stage_b(h_buf, hn_buf, p, act_scr, o_ref):
    for c in range(D_FF // FF_CHUNK):
        yield
        sl = slice(c * FF_CHUNK, (c + 1) * FF_CHUNK)
        g = _dot(hn_buf[...], p["w_gate"][:, sl])
        up = _dot(hn_buf[...], p["w_up"][:, sl])
        act_scr[:, sl] = (g * _sigmoid(g) * up).astype(BF16)
    rows = act_scr.shape[0] // DOWN_ROW_BLOCKS
    cols = D_MODEL // DOWN_COL_BLOCKS
    for r in range(DOWN_ROW_BLOCKS):
        rs = slice(r * rows, (r + 1) * rows)
        ys = []
        for c in range(DOWN_COL_BLOCKS):
            yield
            ys.append(_dot(act_scr[rs, :], p["w_down"][:, c * cols:(c + 1) * cols]))
        y = jnp.concatenate(ys, axis=1)
        o_ref[rs, :] = h_buf[rs, :] + _rms(y, p["n_post_ffn"][...])


_DOWN_PIECE = (5632 // (DOWN_ROW_BLOCKS * DOWN_COL_BLOCKS), 900 // (DOWN_ROW_BLOCKS * DOWN_COL_BLOCKS))
STAGE_B_COST = (((1024, 250),) * (D_FF // FF_CHUNK)
                + (_DOWN_PIECE,) * (DOWN_ROW_BLOCKS * DOWN_COL_BLOCKS))

_AFTER_OTHERS = object()


def _interleave(lead, lead_cost, fill, fill_cost):
    fill_total = sum(m for m, _ in fill_cost)
    valu_total = sum(v for _, v in lead_cost)
    next(lead)
    next(fill)
    fill_i, fill_done, valu_done, marker = 0, 0.0, 0.0, None
    for k in range(len(lead_cost)):
        marker = next(lead)
        valu_done += lead_cost[k][1]
        target = fill_total * valu_done / valu_total
        while fill_i < len(fill_cost) and fill_done + fill_cost[fill_i][0] / 2 <= target:
            next(fill, None)
            fill_done += fill_cost[fill_i][0]
            fill_i += 1
    if marker is not _AFTER_OTHERS:
        raise ValueError("lead_cost does not match the lead generator's phases")
    for _ in range(fill_i, len(fill_cost)):
        next(fill, None)
    leftovers = [next(fill, None), next(lead, None)]
    if leftovers != [None, None]:
        raise ValueError("phase generators yielded more phases than their cost tables list")


_PARAM_ORDER = (
    "n_pre_mix", "w_in", "ln_v_g", "ln_v_b", "w_s", "b_s", "conv_w", "conv_b", "w_gate_lru",
    "b_a", "b_x", "lam", "g_out_gmlp", "g_out_lru", "w_out", "n_post_mix",
    "n_pre_x", "w_q", "kt", "v", "w_o", "n_post_x",
    "n_pre_ffn", "w_gate", "w_up", "w_down", "n_post_ffn",
)


def _layer_kernel(tiles_per_seq, x_ref, *rest):
    n_p = len(_PARAM_ORDER)
    p = dict(zip(_PARAM_ORDER, rest[:n_p]))
    o_ref = rest[n_p]
    h_buf, hn_buf, act_scr, h_carry, xr_scr, a_scr, b_scr, h_scr = rest[n_p + 1:]
    step = pl.program_id(0)
    n_tiles = pl.num_programs(0) - (PIPELINE_DEPTH - 1)
    tile = jnp.minimum(step, n_tiles - 1)

    @pl.when(step == 0)
    def _():
        h_buf[...] = jnp.zeros_like(h_buf)
        hn_buf[...] = jnp.zeros_like(hn_buf)

    @pl.when(lax.rem(tile, tiles_per_seq) == 0)
    def _():
        xr_scr[0:SUBLANES, :] = jnp.zeros((SUBLANES, R_WIDTH), F32)
        h_carry[...] = jnp.zeros_like(h_carry)

    stage_a = _stage_a(x_ref, p, h_buf, hn_buf, h_carry, xr_scr, a_scr, b_scr, h_scr)
    stage_b = _stage_b(h_buf, hn_buf, p, act_scr, o_ref)
    _interleave(stage_a, STAGE_A_COST, stage_b, STAGE_B_COST)


def _resident(shape):
    zeros = (0,) * len(shape)
    return pl.BlockSpec(shape, lambda *_: zeros, pipeline_mode=pl.Buffered(1))


def _block_diag(w):
    h, d, _ = w.shape
    eye = jnp.eye(h, dtype=w.dtype)
    return jnp.einsum('hij,hk->hikj', w, eye).reshape(h * d, h * d)


def _row(p):
    return p.reshape(1, -1).astype(F32)


def kernel(x, mem, w_in, ln_v_g, ln_v_b, w_s, b_s, conv_w, conv_b, w_a, b_a, w_x, b_x, lam,
           g_out_gmlp, g_out_lru, w_out, w_q, w_kv, w_o, w_gate, w_up, w_down, n_pre_mix,
           n_post_mix, n_pre_x, n_mem, n_post_x, n_pre_ffn, n_post_ffn):
    batch, seq, _ = x.shape
    depth = w_in.shape[0]
    t = TOKENS
    assert seq % t == 0 and t % CHUNK == 0
    tiles_per_seq = seq // t
    n_tiles = batch * tiles_per_seq
    heads_per_block = GATE_BLOCK // R_HEAD_DIM

    def tile_of_a(i):
        return jnp.minimum(i, n_tiles - 1)

    def tile_of_b(i):
        return jnp.maximum(i - (PIPELINE_DEPTH - 1), 0)

    x_spec = pl.BlockSpec((None, t, D_MODEL),
                          lambda i: (tile_of_a(i) // tiles_per_seq, tile_of_a(i) % tiles_per_seq, 0))
    o_spec = pl.BlockSpec((None, t, D_MODEL),
                          lambda i: (tile_of_b(i) // tiles_per_seq, tile_of_b(i) % tiles_per_seq, 0))
    kt_spec = pl.BlockSpec((None, D_MODEL, N_MEM), lambda i: (tile_of_a(i) // tiles_per_seq, 0, 0))
    v_spec = pl.BlockSpec((None, N_MEM, D_MODEL), lambda i: (tile_of_a(i) // tiles_per_seq, 0, 0))

    h = x
    for l in range(depth):
        kv_in = [mem, _row(n_mem[l]), w_kv[l].astype(BF16)]
        kt, v = pl.pallas_call(
            _kv_kernel,
            grid=(batch,),
            in_specs=[pl.BlockSpec((None, N_MEM, D_MODEL), lambda b: (b, 0, 0)),
                      _resident(kv_in[1].shape), _resident(kv_in[2].shape)],
            out_specs=[pl.BlockSpec((None, D_MODEL, N_MEM), lambda b: (b, 0, 0)),
                       pl.BlockSpec((None, N_MEM, D_MODEL), lambda b: (b, 0, 0))],
            out_shape=[jax.ShapeDtypeStruct((batch, D_MODEL, N_MEM), BF16),
                       jax.ShapeDtypeStruct((batch, N_MEM, D_MODEL), BF16)],
            compiler_params=pltpu.CompilerParams(dimension_semantics=("arbitrary",),
                                                 vmem_limit_bytes=VMEM_LIMIT_BYTES),
            name="kv_proj",
        )(*kv_in)

        gate_blocks = []
        for blk in range(R_WIDTH // GATE_BLOCK):
            hs = slice(blk * heads_per_block, (blk + 1) * heads_per_block)
            gate_blocks.append(jnp.concatenate(
                [_block_diag(w_a[l, hs]), _block_diag(w_x[l, hs])], axis=1))

        params = {
            "n_pre_mix": _row(n_pre_mix[l]),
            "w_in": w_in[l].astype(BF16),
            "ln_v_g": _row(ln_v_g[l]),
            "ln_v_b": _row(ln_v_b[l]),
            "w_s": w_s[l].astype(BF16),
            "b_s": jnp.repeat(b_s[l].T, G_HEAD_DIM, axis=1).astype(F32),
            "conv_w": conv_w[l].astype(F32),
            "conv_b": _row(conv_b[l]),
            "w_gate_lru": jnp.stack(gate_blocks).astype(BF16),
            "b_a": _row(b_a[l]),
            "b_x": _row(b_x[l]),
            "lam": _row(lam[l]),
            "g_out_gmlp": _row(g_out_gmlp[l]),
            "g_out_lru": _row(g_out_lru[l]),
            "w_out": w_out[l].astype(BF16),
            "n_post_mix": _row(n_post_mix[l]),
            "n_pre_x": _row(n_pre_x[l]),
            "w_q": w_q[l].astype(BF16),
            "kt": kt,
            "v": v,
            "w_o": w_o[l].astype(BF16),
            "n_post_x": _row(n_post_x[l]),
            "n_pre_ffn": _row(n_pre_ffn[l]),
            "w_gate": w_gate[l].astype(BF16),
            "w_up": w_up[l].astype(BF16),
            "w_down": w_down[l].astype(BF16),
            "n_post_ffn": _row(n_post_ffn[l]),
        }
        specs = {"kt": kt_spec, "v": v_spec}
        in_specs = [x_spec] + [specs.get(k) or _resident(params[k].shape) for k in _PARAM_ORDER]

        def layer_body(*refs):
            _layer_kernel(tiles_per_seq, *refs)

        h = pl.pallas_call(
            layer_body,
            grid=(n_tiles + PIPELINE_DEPTH - 1,),
            in_specs=in_specs,
            out_specs=o_spec,
            out_shape=jax.ShapeDtypeStruct((batch, seq, D_MODEL), F32),
            scratch_shapes=[
                pltpu.VMEM((t, D_MODEL), F32),
                pltpu.VMEM((t, D_MODEL), BF16),
                pltpu.VMEM((t, D_FF), BF16),
                pltpu.VMEM((1, R_WIDTH), F32),
                pltpu.VMEM((t + SUBLANES, R_WIDTH), F32),
                pltpu.VMEM((t // SUBLANES, SUBLANES, R_WIDTH), F32),
                pltpu.VMEM((t // SUBLANES, SUBLANES, R_WIDTH), F32),
                pltpu.VMEM((t // SUBLANES, SUBLANES, R_WIDTH), F32),
            ],
            compiler_params=pltpu.CompilerParams(dimension_semantics=("arbitrary",),
                                                 vmem_limit_bytes=VMEM_LIMIT_BYTES),
            name="layer",
        )(h, *[params[k] for k in _PARAM_ORDER])
    return h
```

```python
import math

import jax
import jax.numpy as jnp
from jax import lax
from jax.experimental import pallas as pl
from jax.experimental.pallas import tpu as pltpu

D_MODEL = 1024
G_WIDTH = 512
R_WIDTH = 512
G_HEADS = 4
G_HEAD_DIM = 128
CHUNK = 128
R_HEADS = 8
R_HEAD_DIM = 64
CONV_W = 4
RG_C = 8.0
N_MEM = 256
X_HEADS = 4
X_HEAD_DIM = 256
D_FF = 2816
EPS = 1e-6

SUBLANES = 8
NEG_LOG2E = -math.log2(math.e)
GATE_BLOCK = 256
TOKENS = 512
FF_CHUNK = 256
DOWN_ROW_BLOCKS = 2
DOWN_COL_BLOCKS = 2
OUT_ROW_BLOCKS = 2
PIPELINE_DEPTH = 2
VMEM_LIMIT_BYTES = 60 * 1024 * 1024

BF16 = jnp.bfloat16
F32 = jnp.float32


def _rms(x, g):
    return x * lax.rsqrt(jnp.mean(x * x, axis=-1, keepdims=True) + EPS) * g


def _sigmoid(x):
    return 1.0 / (1.0 + jnp.exp2(x * NEG_LOG2E))


def _gelu(x):
    k1 = 2.0 * math.sqrt(2.0 / math.pi) * NEG_LOG2E
    return x / (1.0 + jnp.exp2(x * (k1 + (k1 * 0.044715) * (x * x))))


def _dot(a, b):
    return jnp.dot(a, b, preferred_element_type=F32)


def _kv_kernel(mem_ref, n_mem_ref, w_kv_ref, kt_ref, v_ref):
    m = _rms(mem_ref[...], n_mem_ref[...]).astype(BF16)
    width = X_HEADS * X_HEAD_DIM
    k = _dot(m, w_kv_ref[:, :width].astype(BF16))
    v = _dot(m, w_kv_ref[:, width:].astype(BF16))
    kt_ref[...] = k.T.astype(BF16)
    v_ref[...] = v.astype(BF16)


def _linear_scan(a, b, h0, a_scr, b_scr, h_scr):
    t, c = a.shape
    groups = t // SUBLANES
    a3 = a.reshape(groups, SUBLANES, c)
    b3 = b.reshape(groups, SUBLANES, c)
    row = lax.broadcasted_iota(jnp.int32, a3.shape, 1)
    shift = 1
    while shift < SUBLANES:
        keep = row >= shift
        a_prev = jnp.where(keep, pltpu.roll(a3, shift, 1), 1.0)
        b_prev = jnp.where(keep, pltpu.roll(b3, shift, 1), 0.0)
        b3 = a3 * b_prev + b3
        a3 = a3 * a_prev
        shift *= 2
    a_scr[...] = a3
    b_scr[...] = b3
    carry = h0
    for g in range(groups):
        blk = a_scr[g] * carry + b_scr[g]
        h_scr[g] = blk
        carry = blk[SUBLANES - 1:SUBLANES, :]
    return h_scr[...].reshape(t, c), carry


def _exchange(blocks):
    sub = lax.broadcasted_iota(jnp.int32, blocks[0].shape, 0)
    v = list(blocks)
    d = SUBLANES // 2
    while d >= 1:
        low = (sub & d) == 0
        nxt = list(v)
        for k in range(SUBLANES):
            if k & d == 0:
                nxt[k] = jnp.where(low, v[k], pltpu.roll(v[k + d], d, 0))
                nxt[k + d] = jnp.where(low, pltpu.roll(v[k], SUBLANES - d, 0), v[k + d])
        v = nxt
        d //= 2
    return v


def _to_segment_layout(x):
    seg_len = x.shape[0] // SUBLANES
    out = []
    for m in range(seg_len // SUBLANES):
        rows = [x[k * seg_len + m * SUBLANES:k * seg_len + (m + 1) * SUBLANES, :]
                for k in range(SUBLANES)]
        out.extend(_exchange(rows))
    return out


def _from_segment_layout(xs):
    seg_len = len(xs)
    groups = [_exchange(xs[m * SUBLANES:(m + 1) * SUBLANES]) for m in range(seg_len // SUBLANES)]
    return jnp.concatenate([groups[m][k] for k in range(SUBLANES)
                            for m in range(seg_len // SUBLANES)], axis=0)


def _stage_a(x_ref, p, h_slot, hn_buf, h_carry, xr_scr, a_scr, b_scr, h_scr):
    yield
    x = x_ref[...]
    t = x.shape[0]
    w_in_ref = p["w_in"]
    hn = _rms(x, p["n_pre_mix"][...]).astype(BF16)

    yield
    u = _gelu(_dot(hn, w_in_ref[:, 0:G_WIDTH]))
    yield
    gv = _gelu(_dot(hn, w_in_ref[:, G_WIDTH:2 * G_WIDTH]))
    mu = jnp.mean(gv, axis=-1, keepdims=True)
    cen = gv - mu
    var = jnp.mean(cen * cen, axis=-1, keepdims=True)
    v = (cen * lax.rsqrt(var + EPS) * p["ln_v_g"][...] + p["ln_v_b"][...]).astype(BF16)
    yield
    tri_r = lax.broadcasted_iota(jnp.int32, (CHUNK, CHUNK), 0)
    tri_c = lax.broadcasted_iota(jnp.int32, (CHUNK, CHUNK), 1)
    causal = tri_r >= tri_c
    ws = [jnp.where(causal, p["w_s"][hd], jnp.zeros((CHUNK, CHUNK), BF16)) for hd in range(G_HEADS)]
    bs = p["b_s"][...]
    rows = []
    for c in range(t // CHUNK):
        cols = []
        for hd in range(G_HEADS):
            vb = v[c * CHUNK:(c + 1) * CHUNK, hd * G_HEAD_DIM:(hd + 1) * G_HEAD_DIM]
            cols.append(_dot(ws[hd], vb))
        rows.append(jnp.concatenate(cols, axis=1) + bs)
    s = jnp.concatenate(rows, axis=0)
    y_g = _rms(u * s, p["g_out_gmlp"][...]).astype(BF16)

    yield
    xr = _dot(hn, w_in_ref[:, 2 * G_WIDTH:2 * G_WIDTH + R_WIDTH])
    xr_scr[SUBLANES:SUBLANES + t, :] = xr
    cw = p["conv_w"][...]
    xc = p["conv_b"][...]
    for k in range(CONV_W):
        off = SUBLANES - (CONV_W - 1) + k
        xc = xc + xr_scr[off:off + t, :] * cw[k:k + 1, :]
    xr_scr[0:SUBLANES, :] = xr[t - SUBLANES:t, :]
    xc_b = xc.astype(BF16)
    yield
    ga, gx = [], []
    for blk in range(R_WIDTH // GATE_BLOCK):
        g2 = _dot(xc_b[:, blk * GATE_BLOCK:(blk + 1) * GATE_BLOCK], p["w_gate_lru"][blk])
        ga.append(g2[:, :GATE_BLOCK])
        gx.append(g2[:, GATE_BLOCK:])
    r = _sigmoid(jnp.concatenate(ga, axis=1) + p["b_a"][...])
    i = _sigmoid(jnp.concatenate(gx, axis=1) + p["b_x"][...])
    nl = -p["lam"][...]
    softplus = jnp.maximum(nl, 0.0) + jnp.log1p(jnp.exp(-jnp.abs(nl)))
    neg_log_a = r * (RG_C * softplus)
    a = jnp.exp2(neg_log_a * NEG_LOG2E)
    mult = jnp.sqrt(jnp.tanh(neg_log_a) * (a * a + 1.0))
    bterm = mult * (i * xc)
    yield
    hseq, h_last = _linear_scan(a, bterm, h_carry[...], a_scr, b_scr, h_scr)
    h_carry[...] = h_last
    yield
    gr = _dot(hn, w_in_ref[:, 2 * G_WIDTH + R_WIDTH:])
    y_r = _rms(_gelu(gr) * hseq, p["g_out_lru"][...]).astype(BF16)
    yield
    y = _dot(jnp.concatenate([y_g, y_r], axis=1), p["w_out"][...])
    h1 = x + _rms(y, p["n_post_mix"][...])

    yield
    hn = _rms(h1, p["n_pre_x"][...]).astype(BF16)
    q = _dot(hn, p["w_q"][...]).astype(BF16)
    outs = []
    for hd in range(X_HEADS):
        if hd % 2 == 0:
            yield
        sl = slice(hd * X_HEAD_DIM, (hd + 1) * X_HEAD_DIM)
        sc = _dot(q[:, sl], p["kt"][sl, :]) * (X_HEAD_DIM ** -0.5)
        m = jnp.max(sc, axis=-1, keepdims=True)
        e = jnp.exp(sc - m)
        pr = (e / jnp.sum(e, axis=-1, keepdims=True)).astype(BF16)
        outs.append(_dot(pr, p["v"][:, sl]).astype(BF16))
    o_cat = jnp.concatenate(outs, axis=1)
    rows = t // OUT_ROW_BLOCKS
    h2s, hn2s = [], []
    for r in range(OUT_ROW_BLOCKS):
        yield
        rs = slice(r * rows, (r + 1) * rows)
        y = _dot(o_cat[rs, :], p["w_o"][...])
        h2s.append(h1[rs, :] + _rms(y, p["n_post_x"][...]))
        hn2s.append(_rms(h2s[-1], p["n_pre_ffn"][...]).astype(BF16))
    yield _AFTER_OTHERS
    h_slot[...] = jnp.concatenate(h2s, axis=0)
    hn_buf[...] = jnp.concatenate(hn2s, axis=0)


STAGE_A_COST = (
    (0, 800),
    (1024, 500),
    (1024, 1000),
    (500, 600),
    (1024, 700),
    (512, 1800),
    (0, 800),
    (1024, 1100),
    (2048, 900),
    (2048, 1000),
    (512, 700),
    (512, 700),
) + ((2048 // OUT_ROW_BLOCKS, 1700 // OUT_ROW_BLOCKS),) * OUT_ROW_BLOCKS


def _stage_b(h_buf, hn_buf, p, act_scr, o_ref):
    for c in range(D_FF // FF_CHUNK):
        yield
        sl = slice(c * FF_CHUNK, (c + 1) * FF_CHUNK)
        g = _dot(hn_buf[...], p["w_gate"][:, sl])
        up = _dot(hn_buf[...], p["w_up"][:, sl])
        act_scr[:, sl] = (g * _sigmoid(g) * up).astype(BF16)
    rows = act_scr.shape[0] // DOWN_ROW_BLOCKS
    cols = D_MODEL // DOWN_COL_BLOCKS
    for r in range(DOWN_ROW_BLOCKS):
        rs = slice(r * rows, (r + 1) * rows)
        ys = []
        for c in range(DOWN_COL_BLOCKS):
            yield
            ys.append(_dot(act_scr[rs, :], p["w_down"][:, c * cols:(c + 1) * cols]))
        y = jnp.concatenate(ys, axis=1)
        o_ref[rs, :] = h_buf[rs, :] + _rms(y, p["n_post_ffn"][...])


_DOWN_PIECE = (5632 // (DOWN_ROW_BLOCKS * DOWN_COL_BLOCKS), 900 // (DOWN_ROW_BLOCKS * DOWN_COL_BLOCKS))
STAGE_B_COST = (((1024, 250),) * (D_FF // FF_CHUNK)
                + (_DOWN_PIECE,) * (DOWN_ROW_BLOCKS * DOWN_COL_BLOCKS))

_AFTER_OTHERS = object()


def _interleave(lead, lead_cost, fill, fill_cost):
    fill_total = sum(m for m, _ in fill_cost)
    valu_total = sum(v for _, v in lead_cost)
    next(lead)
    next(fill)
    fill_i, fill_done, valu_done, marker = 0, 0.0, 0.0, None
    for k in range(len(lead_cost)):
        marker = next(lead)
        valu_done += lead_cost[k][1]
        target = fill_total * valu_done / valu_total
        while fill_i < len(fill_cost) and fill_done + fill_cost[fill_i][0] / 2 <= target:
            next(fill, None)
            fill_done += fill_cost[fill_i][0]
            fill_i += 1
    if marker is not _AFTER_OTHERS:
        raise ValueError("lead_cost does not match the lead generator's phases")
    for _ in range(fill_i, len(fill_cost)):
        next(fill, None)
    leftovers = [next(fill, None), next(lead, None)]
    if leftovers != [None, None]:
        raise ValueError("phase generators yielded more phases than their cost tables list")


_PARAM_ORDER = (
    "n_pre_mix", "w_in", "ln_v_g", "ln_v_b", "w_s", "b_s", "conv_w", "conv_b", "w_gate_lru",
    "b_a", "b_x", "lam", "g_out_gmlp", "g_out_lru", "w_out", "n_post_mix",
    "n_pre_x", "w_q", "kt", "v", "w_o", "n_post_x",
    "n_pre_ffn", "w_gate", "w_up", "w_down", "n_post_ffn",
)
_MATMUL_WEIGHTS = ("w_in", "w_out", "w_q", "w_o", "w_gate", "w_up", "w_down")
LOAD_ROWS = 128


def _load_weights(hbm_refs, vmem_refs, stage, sems):
    jobs = [(src, dst, r0) for src, dst in zip(hbm_refs, vmem_refs)
            for r0 in range(0, src.shape[0], LOAD_ROWS)]

    def copy(n):
        src, _, r0 = jobs[n]
        return pltpu.make_async_copy(src.at[pl.ds(r0, LOAD_ROWS), :],
                                     stage.at[n % 2, :, pl.ds(0, src.shape[1])], sems.at[n % 2])

    copy(0).start()
    for n, (src, dst, r0) in enumerate(jobs):
        if n + 1 < len(jobs):
            copy(n + 1).start()
        copy(n).wait()
        dst[pl.ds(r0, LOAD_ROWS), :] = stage[n % 2, :, 0:src.shape[1]].astype(BF16)


def _layer_kernel(tiles_per_seq, x_ref, *rest):
    n_p, n_w = len(_PARAM_ORDER), len(_MATMUL_WEIGHTS)
    p = dict(zip(_PARAM_ORDER, rest[:n_p]))
    o_ref = rest[n_p]
    scratch = rest[n_p + 1:]
    weights, (w_stage, w_sems) = scratch[:n_w], scratch[n_w:n_w + 2]
    h_buf, hn_buf, act_scr, h_carry, xr_scr, a_scr, b_scr, h_scr = scratch[n_w + 2:]
    step = pl.program_id(0)
    n_tiles = pl.num_programs(0) - (PIPELINE_DEPTH - 1)
    tile = jnp.minimum(step, n_tiles - 1)

    @pl.when(step == 0)
    def _():
        _load_weights([p[k] for k in _MATMUL_WEIGHTS], weights, w_stage, w_sems)
        h_buf[...] = jnp.zeros_like(h_buf)
        hn_buf[...] = jnp.zeros_like(hn_buf)

    p.update(zip(_MATMUL_WEIGHTS, weights))

    @pl.when(lax.rem(tile, tiles_per_seq) == 0)
    def _():
        xr_scr[0:SUBLANES, :] = jnp.zeros((SUBLANES, R_WIDTH), F32)
        h_carry[...] = jnp.zeros_like(h_carry)

    stage_a = _stage_a(x_ref, p, h_buf, hn_buf, h_carry, xr_scr, a_scr, b_scr, h_scr)
    stage_b = _stage_b(h_buf, hn_buf, p, act_scr, o_ref)
    _interleave(stage_a, STAGE_A_COST, stage_b, STAGE_B_COST)


def _resident(shape):
    zeros = (0,) * len(shape)
    return pl.BlockSpec(shape, lambda *_: zeros, pipeline_mode=pl.Buffered(1))


def _block_diag(w):
    h, d, _ = w.shape
    eye = jnp.eye(h, dtype=w.dtype)
    return jnp.einsum('hij,hk->hikj', w, eye).reshape(h * d, h * d)


def _row(p):
    return p.reshape(1, -1).astype(F32)


def kernel(x, mem, w_in, ln_v_g, ln_v_b, w_s, b_s, conv_w, conv_b, w_a, b_a, w_x, b_x, lam,
           g_out_gmlp, g_out_lru, w_out, w_q, w_kv, w_o, w_gate, w_up, w_down, n_pre_mix,
           n_post_mix, n_pre_x, n_mem, n_post_x, n_pre_ffn, n_post_ffn):
    batch, seq, _ = x.shape
    depth = w_in.shape[0]
    t = TOKENS
    assert seq % t == 0 and t % CHUNK == 0
    tiles_per_seq = seq // t
    n_tiles = batch * tiles_per_seq
    heads_per_block = GATE_BLOCK // R_HEAD_DIM

    def tile_of_a(i):
        return jnp.minimum(i, n_tiles - 1)

    def tile_of_b(i):
        return jnp.maximum(i - (PIPELINE_DEPTH - 1), 0)

    x_spec = pl.BlockSpec((None, t, D_MODEL),
                          lambda i: (tile_of_a(i) // tiles_per_seq, tile_of_a(i) % tiles_per_seq, 0))
    o_spec = pl.BlockSpec((None, t, D_MODEL),
                          lambda i: (tile_of_b(i) // tiles_per_seq, tile_of_b(i) % tiles_per_seq, 0))
    kt_spec = pl.BlockSpec((None, D_MODEL, N_MEM), lambda i: (tile_of_a(i) // tiles_per_seq, 0, 0))
    v_spec = pl.BlockSpec((None, N_MEM, D_MODEL), lambda i: (tile_of_a(i) // tiles_per_seq, 0, 0))

    h = x
    for l in range(depth):
        kv_in = [mem, _row(n_mem[l]), w_kv[l]]
        kt, v = pl.pallas_call(
            _kv_kernel,
            grid=(batch,),
            in_specs=[pl.BlockSpec((None, N_MEM, D_MODEL), lambda b: (b, 0, 0)),
                      _resident(kv_in[1].shape), _resident(kv_in[2].shape)],
            out_specs=[pl.BlockSpec((None, D_MODEL, N_MEM), lambda b: (b, 0, 0)),
                       pl.BlockSpec((None, N_MEM, D_MODEL), lambda b: (b, 0, 0))],
            out_shape=[jax.ShapeDtypeStruct((batch, D_MODEL, N_MEM), BF16),
                       jax.ShapeDtypeStruct((batch, N_MEM, D_MODEL), BF16)],
            compiler_params=pltpu.CompilerParams(dimension_semantics=("arbitrary",),
                                                 vmem_limit_bytes=VMEM_LIMIT_BYTES),
            name="kv_proj",
        )(*kv_in)

        gate_blocks = []
        for blk in range(R_WIDTH // GATE_BLOCK):
            hs = slice(blk * heads_per_block, (blk + 1) * heads_per_block)
            gate_blocks.append(jnp.concatenate(
                [_block_diag(w_a[l, hs]), _block_diag(w_x[l, hs])], axis=1))

        params = {
            "n_pre_mix": _row(n_pre_mix[l]),
            "w_in": w_in[l],
            "ln_v_g": _row(ln_v_g[l]),
            "ln_v_b": _row(ln_v_b[l]),
            "w_s": w_s[l].astype(BF16),
            "b_s": jnp.repeat(b_s[l].T, G_HEAD_DIM, axis=1).astype(F32),
            "conv_w": conv_w[l].astype(F32),
            "conv_b": _row(conv_b[l]),
            "w_gate_lru": jnp.stack(gate_blocks).astype(BF16),
            "b_a": _row(b_a[l]),
            "b_x": _row(b_x[l]),
            "lam": _row(lam[l]),
            "g_out_gmlp": _row(g_out_gmlp[l]),
            "g_out_lru": _row(g_out_lru[l]),
            "w_out": w_out[l],
            "n_post_mix": _row(n_post_mix[l]),
            "n_pre_x": _row(n_pre_x[l]),
            "w_q": w_q[l],
            "kt": kt,
            "v": v,
            "w_o": w_o[l],
            "n_post_x": _row(n_post_x[l]),
            "n_pre_ffn": _row(n_pre_ffn[l]),
            "w_gate": w_gate[l],
            "w_up": w_up[l],
            "w_down": w_down[l],
            "n_post_ffn": _row(n_post_ffn[l]),
        }
        specs = {"kt": kt_spec, "v": v_spec}
        specs.update({k: pl.BlockSpec(memory_space=pl.ANY) for k in _MATMUL_WEIGHTS})
        in_specs = [x_spec] + [specs.get(k) or _resident(params[k].shape) for k in _PARAM_ORDER]
        stage_cols = max(params[k].shape[1] for k in _MATMUL_WEIGHTS)
        assert all(params[k].shape[0] % LOAD_ROWS == 0 for k in _MATMUL_WEIGHTS)

        def layer_body(*refs):
            _layer_kernel(tiles_per_seq, *refs)

        h = pl.pallas_call(
            layer_body,
            grid=(n_tiles + PIPELINE_DEPTH - 1,),
            in_specs=in_specs,
            out_specs=o_spec,
            out_shape=jax.ShapeDtypeStruct((batch, seq, D_MODEL), F32),
            scratch_shapes=[pltpu.VMEM(params[k].shape, BF16) for k in _MATMUL_WEIGHTS] + [
                pltpu.VMEM((2, LOAD_ROWS, stage_cols), F32),
                pltpu.SemaphoreType.DMA((2,)),
                pltpu.VMEM((t, D_MODEL), F32),
                pltpu.VMEM((t, D_MODEL), BF16),
                pltpu.VMEM((t, D_FF), BF16),
                pltpu.VMEM((1, R_WIDTH), F32),
                pltpu.VMEM((t + SUBLANES, R_WIDTH), F32),
                pltpu.VMEM((t // SUBLANES, SUBLANES, R_WIDTH), F32),
                pltpu.VMEM((t // SUBLANES, SUBLANES, R_WIDTH), F32),
                pltpu.VMEM((t // SUBLANES, SUBLANES, R_WIDTH), F32),
            ],
            compiler_params=pltpu.CompilerParams(dimension_semantics=("arbitrary",),
                                                 vmem_limit_bytes=VMEM_LIMIT_BYTES),
            name="layer",
        )(h, *[params[k] for k in _PARAM_ORDER])
    return h
```

```python
import math

import jax
import jax.numpy as jnp
from jax import lax
from jax.experimental import pallas as pl
from jax.experimental.pallas import tpu as pltpu

D_MODEL = 1024
G_WIDTH = 512
R_WIDTH = 512
G_HEADS = 4
G_HEAD_DIM = 128
CHUNK = 128
R_HEADS = 8
R_HEAD_DIM = 64
CONV_W = 4
RG_C = 8.0
N_MEM = 256
X_HEADS = 4
X_HEAD_DIM = 256
D_FF = 2816
EPS = 1e-6

SUBLANES = 8
NEG_LOG2E = -math.log2(math.e)
GATE_BLOCK = 256
TOKENS = 512
FF_CHUNK = 256
DOWN_ROW_BLOCKS = 2
DOWN_COL_BLOCKS = 2
OUT_ROW_BLOCKS = 2
PIPELINE_DEPTH = 2
VMEM_LIMIT_BYTES = 60 * 1024 * 1024

BF16 = jnp.bfloat16
F32 = jnp.float32


def _rms(x, g):
    return x * lax.rsqrt(jnp.mean(x * x, axis=-1, keepdims=True) + EPS) * g


def _sigmoid(x):
    return 1.0 / (1.0 + jnp.exp2(x * NEG_LOG2E))


def _gelu(x):
    k1 = 2.0 * math.sqrt(2.0 / math.pi) * NEG_LOG2E
    return x / (1.0 + jnp.exp2(x * (k1 + (k1 * 0.044715) * (x * x))))


def _dot(a, b):
    return jnp.dot(a, b, preferred_element_type=F32)


def _kv_kernel(mem_ref, n_mem_ref, w_kv_ref, kt_ref, v_ref):
    m = _rms(mem_ref[...], n_mem_ref[...]).astype(BF16)
    width = X_HEADS * X_HEAD_DIM
    k = _dot(m, w_kv_ref[:, :width].astype(BF16))
    v = _dot(m, w_kv_ref[:, width:].astype(BF16))
    kt_ref[...] = k.T.astype(BF16)
    v_ref[...] = v.astype(BF16)


def _linear_scan(a, b, h0, a_scr, b_scr, h_scr):
    t, c = a.shape
    groups = t // SUBLANES
    a3 = a.reshape(groups, SUBLANES, c)
    b3 = b.reshape(groups, SUBLANES, c)
    row = lax.broadcasted_iota(jnp.int32, a3.shape, 1)
    shift = 1
    while shift < SUBLANES:
        keep = row >= shift
        a_prev = jnp.where(keep, pltpu.roll(a3, shift, 1), 1.0)
        b_prev = jnp.where(keep, pltpu.roll(b3, shift, 1), 0.0)
        b3 = a3 * b_prev + b3
        a3 = a3 * a_prev
        shift *= 2
    a_scr[...] = a3
    b_scr[...] = b3
    carry = h0
    for g in range(groups):
        blk = a_scr[g] * carry + b_scr[g]
        h_scr[g] = blk
        carry = blk[SUBLANES - 1:SUBLANES, :]
    return h_scr[...].reshape(t, c), carry


def _exchange(blocks):
    sub = lax.broadcasted_iota(jnp.int32, blocks[0].shape, 0)
    v = list(blocks)
    d = SUBLANES // 2
    while d >= 1:
        low = (sub & d) == 0
        nxt = list(v)
        for k in range(SUBLANES):
            if k & d == 0:
                nxt[k] = jnp.where(low, v[k], pltpu.roll(v[k + d], d, 0))
                nxt[k + d] = jnp.where(low, pltpu.roll(v[k], SUBLANES - d, 0), v[k + d])
        v = nxt
        d //= 2
    return v


def _to_segment_layout(x):
    seg_len = x.shape[0] // SUBLANES
    out = []
    for m in range(seg_len // SUBLANES):
        rows = [x[k * seg_len + m * SUBLANES:k * seg_len + (m + 1) * SUBLANES, :]
                for k in range(SUBLANES)]
        out.extend(_exchange(rows))
    return out


def _from_segment_layout(xs):
    seg_len = len(xs)
    groups = [_exchange(xs[m * SUBLANES:(m + 1) * SUBLANES]) for m in range(seg_len // SUBLANES)]
    return jnp.concatenate([groups[m][k] for k in range(SUBLANES)
                            for m in range(seg_len // SUBLANES)], axis=0)


def _stage_a(x_ref, p, h_slot, hn_buf, h_carry, xr_scr, a_scr, b_scr, h_scr):
    yield
    x = x_ref[...]
    t = x.shape[0]
    w_in_ref = p["w_in"]
    hn = _rms(x, p["n_pre_mix"][...]).astype(BF16)

    yield
    u = _gelu(_dot(hn, w_in_ref[:, 0:G_WIDTH]))
    yield
    gv = _gelu(_dot(hn, w_in_ref[:, G_WIDTH:2 * G_WIDTH]))
    mu = jnp.mean(gv, axis=-1, keepdims=True)
    cen = gv - mu
    var = jnp.mean(cen * cen, axis=-1, keepdims=True)
    v = (cen * lax.rsqrt(var + EPS) * p["ln_v_g"][...] + p["ln_v_b"][...]).astype(BF16)
    yield
    tri_r = lax.broadcasted_iota(jnp.int32, (CHUNK, CHUNK), 0)
    tri_c = lax.broadcasted_iota(jnp.int32, (CHUNK, CHUNK), 1)
    causal = tri_r >= tri_c
    ws = [jnp.where(causal, p["w_s"][hd], jnp.zeros((CHUNK, CHUNK), BF16)) for hd in range(G_HEADS)]
    bs = p["b_s"][...]
    rows = []
    for c in range(t // CHUNK):
        cols = []
        for hd in range(G_HEADS):
            vb = v[c * CHUNK:(c + 1) * CHUNK, hd * G_HEAD_DIM:(hd + 1) * G_HEAD_DIM]
            cols.append(_dot(ws[hd], vb))
        rows.append(jnp.concatenate(cols, axis=1) + bs)
    s = jnp.concatenate(rows, axis=0)
    y_g = _rms(u * s, p["g_out_gmlp"][...]).astype(BF16)

    yield
    xr = _dot(hn, w_in_ref[:, 2 * G_WIDTH:2 * G_WIDTH + R_WIDTH])
    xr_scr[SUBLANES:SUBLANES + t, :] = xr
    cw = p["conv_w"][...]
    xc = p["conv_b"][...]
    for k in range(CONV_W):
        off = SUBLANES - (CONV_W - 1) + k
        xc = xc + xr_scr[off:off + t, :] * cw[k:k + 1, :]
    xr_scr[0:SUBLANES, :] = xr[t - SUBLANES:t, :]
    xc_b = xc.astype(BF16)
    yield
    ga, gx = [], []
    for blk in range(R_WIDTH // GATE_BLOCK):
        g2 = _dot(xc_b[:, blk * GATE_BLOCK:(blk + 1) * GATE_BLOCK], p["w_gate_lru"][blk])
        ga.append(g2[:, :GATE_BLOCK])
        gx.append(g2[:, GATE_BLOCK:])
    r = _sigmoid(jnp.concatenate(ga, axis=1) + p["b_a"][...])
    i = _sigmoid(jnp.concatenate(gx, axis=1) + p["b_x"][...])
    nl = -p["lam"][...]
    softplus = jnp.maximum(nl, 0.0) + jnp.log1p(jnp.exp(-jnp.abs(nl)))
    neg_log_a = r * (RG_C * softplus)
    a = jnp.exp2(neg_log_a * NEG_LOG2E)
    mult = jnp.sqrt(jnp.tanh(neg_log_a) * (a * a + 1.0))
    bterm = mult * (i * xc)
    yield
    hseq, h_last = _linear_scan(a, bterm, h_carry[...], a_scr, b_scr, h_scr)
    h_carry[...] = h_last
    yield
    gr = _dot(hn, w_in_ref[:, 2 * G_WIDTH + R_WIDTH:])
    y_r = _rms(_gelu(gr) * hseq, p["g_out_lru"][...]).astype(BF16)
    yield
    y = _dot(jnp.concatenate([y_g, y_r], axis=1), p["w_out"][...])
    h1 = x + _rms(y, p["n_post_mix"][...])

    yield
    hn = _rms(h1, p["n_pre_x"][...]).astype(BF16)
    q = _dot(hn, p["w_q"][...]).astype(BF16)
    outs = []
    for hd in range(X_HEADS):
        if hd % 2 == 0:
            yield
        sl = slice(hd * X_HEAD_DIM, (hd + 1) * X_HEAD_DIM)
        sc = _dot(q[:, sl], p["kt"][sl, :]) * (X_HEAD_DIM ** -0.5)
        m = jnp.max(sc, axis=-1, keepdims=True)
        e = jnp.exp(sc - m)
        pr = (e / jnp.sum(e, axis=-1, keepdims=True)).astype(BF16)
        outs.append(_dot(pr, p["v"][:, sl]).astype(BF16))
    o_cat = jnp.concatenate(outs, axis=1)
    rows = t // OUT_ROW_BLOCKS
    h2s, hn2s = [], []
    for r in range(OUT_ROW_BLOCKS):
        yield
        rs = slice(r * rows, (r + 1) * rows)
        y = _dot(o_cat[rs, :], p["w_o"][...])
        h2s.append(h1[rs, :] + _rms(y, p["n_post_x"][...]))
        hn2s.append(_rms(h2s[-1], p["n_pre_ffn"][...]).astype(BF16))
    yield _AFTER_OTHERS
    h_slot[...] = jnp.concatenate(h2s, axis=0)
    hn_buf[...] = jnp.concatenate(hn2s, axis=0)


STAGE_A_COST = (
    (0, 800),
    (1024, 500),
    (1024, 1000),
    (500, 600),
    (1024, 700),
    (512, 1800),
    (0, 800),
    (1024, 1100),
    (2048, 900),
    (2048, 1000),
    (512, 700),
    (512, 700),
) + ((2048 // OUT_ROW_BLOCKS, 1700 // OUT_ROW_BLOCKS),) * OUT_ROW_BLOCKS


def _stage_b(h_buf, hn_buf, p, act_scr, o_ref):
    for c in range(D_FF // FF_CHUNK):
        yield
        sl = slice(c * FF_CHUNK, (c + 1) * FF_CHUNK)
        g = _dot(hn_buf[...], p["w_gate"][:, sl])
        up = _dot(hn_buf[...], p["w_up"][:, sl])
        act_scr[:, sl] = (g * _sigmoid(g) * up).astype(BF16)
    rows = act_scr.shape[0] // DOWN_ROW_BLOCKS
    cols = D_MODEL // DOWN_COL_BLOCKS
    for r in range(DOWN_ROW_BLOCKS):
        rs = slice(r * rows, (r + 1) * rows)
        ys = []
        for c in range(DOWN_COL_BLOCKS):
            yield
            ys.append(_dot(act_scr[rs, :], p["w_down"][:, c * cols:(c + 1) * cols]))
        y = jnp.concatenate(ys, axis=1)
        o_ref[rs, :] = h_buf[rs, :] + _rms(y, p["n_post_ffn"][...])


_DOWN_PIECE = (5632 // (DOWN_ROW_BLOCKS * DOWN_COL_BLOCKS), 900 // (DOWN_ROW_BLOCKS * DOWN_COL_BLOCKS))
STAGE_B_COST = (((1024, 250),) * (D_FF // FF_CHUNK)
                + (_DOWN_PIECE,) * (DOWN_ROW_BLOCKS * DOWN_COL_BLOCKS))

_AFTER_OTHERS = object()


def _interleave(lead, lead_cost, fill, fill_cost):
    fill_total = sum(m for m, _ in fill_cost)
    valu_total = sum(v for _, v in lead_cost)
    next(lead)
    next(fill)
    fill_i, fill_done, valu_done, marker = 0, 0.0, 0.0, None
    for k in range(len(lead_cost)):
        marker = next(lead)
        valu_done += lead_cost[k][1]
        target = fill_total * valu_done / valu_total
        while fill_i < len(fill_cost) and fill_done + fill_cost[fill_i][0] / 2 <= target:
            next(fill, None)
            fill_done += fill_cost[fill_i][0]
            fill_i += 1
    if marker is not _AFTER_OTHERS:
        raise ValueError("lead_cost does not match the lead generator's phases")
    for _ in range(fill_i, len(fill_cost)):
        next(fill, None)
    leftovers = [next(fill, None), next(lead, None)]
    if leftovers != [None, None]:
        raise ValueError("phase generators yielded more phases than their cost tables list")


_PARAM_ORDER = (
    "n_pre_mix", "w_in", "ln_v_g", "ln_v_b", "w_s", "b_s", "conv_w", "conv_b", "w_gate_lru",
    "b_a", "b_x", "lam", "g_out_gmlp", "g_out_lru", "w_out", "n_post_mix",
    "n_pre_x", "w_q", "kt", "v", "w_o", "n_post_x",
    "n_pre_ffn", "w_gate", "w_up", "w_down", "n_post_ffn",
)
_MATMUL_WEIGHTS = ("w_in", "w_out", "w_q", "w_o", "w_gate", "w_up", "w_down")
LOAD_ROWS = 128
LOAD_SLOTS = 4


def _load_weights(hbm_refs, vmem_refs, stage, sems):
    jobs = [(src, dst, r0) for src, dst in zip(hbm_refs, vmem_refs)
            for r0 in range(0, src.shape[0], LOAD_ROWS)]

    def copy(n):
        src, _, r0 = jobs[n]
        slot = n % LOAD_SLOTS
        return pltpu.make_async_copy(src.at[pl.ds(r0, LOAD_ROWS), :],
                                     stage.at[slot, :, pl.ds(0, src.shape[1])], sems.at[slot])

    ahead = LOAD_SLOTS - 1
    for n in range(min(ahead, len(jobs))):
        copy(n).start()
    for n, (src, dst, r0) in enumerate(jobs):
        if n + ahead < len(jobs):
            copy(n + ahead).start()
        copy(n).wait()
        dst[pl.ds(r0, LOAD_ROWS), :] = stage[n % LOAD_SLOTS, :, 0:src.shape[1]].astype(BF16)


def _layer_kernel(tiles_per_seq, x_ref, *rest):
    n_p, n_w = len(_PARAM_ORDER), len(_MATMUL_WEIGHTS)
    p = dict(zip(_PARAM_ORDER, rest[:n_p]))
    o_ref = rest[n_p]
    scratch = rest[n_p + 1:]
    weights, (w_stage, w_sems) = scratch[:n_w], scratch[n_w:n_w + 2]
    h_buf, hn_buf, act_scr, h_carry, xr_scr, a_scr, b_scr, h_scr = scratch[n_w + 2:]
    step = pl.program_id(0)
    n_tiles = pl.num_programs(0) - (PIPELINE_DEPTH - 1)
    tile = jnp.minimum(step, n_tiles - 1)

    @pl.when(step == 0)
    def _():
        _load_weights([p[k] for k in _MATMUL_WEIGHTS], weights, w_stage, w_sems)
        h_buf[...] = jnp.zeros_like(h_buf)
        hn_buf[...] = jnp.zeros_like(hn_buf)

    p.update(zip(_MATMUL_WEIGHTS, weights))

    @pl.when(lax.rem(tile, tiles_per_seq) == 0)
    def _():
        xr_scr[0:SUBLANES, :] = jnp.zeros((SUBLANES, R_WIDTH), F32)
        h_carry[...] = jnp.zeros_like(h_carry)

    stage_a = _stage_a(x_ref, p, h_buf, hn_buf, h_carry, xr_scr, a_scr, b_scr, h_scr)
    stage_b = _stage_b(h_buf, hn_buf, p, act_scr, o_ref)
    _interleave(stage_a, STAGE_A_COST, stage_b, STAGE_B_COST)


def _resident(shape):
    zeros = (0,) * len(shape)
    return pl.BlockSpec(shape, lambda *_: zeros, pipeline_mode=pl.Buffered(1))


def _block_diag(w):
    h, d, _ = w.shape
    eye = jnp.eye(h, dtype=w.dtype)
    return jnp.einsum('hij,hk->hikj', w, eye).reshape(h * d, h * d)


def _row(p):
    return p.reshape(1, -1).astype(F32)


def kernel(x, mem, w_in, ln_v_g, ln_v_b, w_s, b_s, conv_w, conv_b, w_a, b_a, w_x, b_x, lam,
           g_out_gmlp, g_out_lru, w_out, w_q, w_kv, w_o, w_gate, w_up, w_down, n_pre_mix,
           n_post_mix, n_pre_x, n_mem, n_post_x, n_pre_ffn, n_post_ffn):
    batch, seq, _ = x.shape
    depth = w_in.shape[0]
    t = TOKENS
    assert seq % t == 0 and t % CHUNK == 0
    tiles_per_seq = seq // t
    n_tiles = batch * tiles_per_seq
    heads_per_block = GATE_BLOCK // R_HEAD_DIM

    def tile_of_a(i):
        return jnp.minimum(i, n_tiles - 1)

    def tile_of_b(i):
        return jnp.maximum(i - (PIPELINE_DEPTH - 1), 0)

    x_spec = pl.BlockSpec((None, t, D_MODEL),
                          lambda i: (tile_of_a(i) // tiles_per_seq, tile_of_a(i) % tiles_per_seq, 0))
    o_spec = pl.BlockSpec((None, t, D_MODEL),
                          lambda i: (tile_of_b(i) // tiles_per_seq, tile_of_b(i) % tiles_per_seq, 0))
    kt_spec = pl.BlockSpec((None, D_MODEL, N_MEM), lambda i: (tile_of_a(i) // tiles_per_seq, 0, 0))
    v_spec = pl.BlockSpec((None, N_MEM, D_MODEL), lambda i: (tile_of_a(i) // tiles_per_seq, 0, 0))

    h = x
    for l in range(depth):
        kv_in = [mem, _row(n_mem[l]), w_kv[l]]
        kt, v = pl.pallas_call(
            _kv_kernel,
            grid=(batch,),
            in_specs=[pl.BlockSpec((None, N_MEM, D_MODEL), lambda b: (b, 0, 0)),
                      _resident(kv_in[1].shape), _resident(kv_in[2].shape)],
            out_specs=[pl.BlockSpec((None, D_MODEL, N_MEM), lambda b: (b, 0, 0)),
                       pl.BlockSpec((None, N_MEM, D_MODEL), lambda b: (b, 0, 0))],
            out_shape=[jax.ShapeDtypeStruct((batch, D_MODEL, N_MEM), BF16),
                       jax.ShapeDtypeStruct((batch, N_MEM, D_MODEL), BF16)],
            compiler_params=pltpu.CompilerParams(dimension_semantics=("arbitrary",),
                                                 vmem_limit_bytes=VMEM_LIMIT_BYTES),
            name="kv_proj",
        )(*kv_in)

        gate_blocks = []
        for blk in range(R_WIDTH // GATE_BLOCK):
            hs = slice(blk * heads_per_block, (blk + 1) * heads_per_block)
            gate_blocks.append(jnp.concatenate(
                [_block_diag(w_a[l, hs]), _block_diag(w_x[l, hs])], axis=1))

        params = {
            "n_pre_mix": _row(n_pre_mix[l]),
            "w_in": w_in[l],
            "ln_v_g": _row(ln_v_g[l]),
            "ln_v_b": _row(ln_v_b[l]),
            "w_s": w_s[l].astype(BF16),
            "b_s": jnp.repeat(b_s[l].T, G_HEAD_DIM, axis=1).astype(F32),
            "conv_w": conv_w[l].astype(F32),
            "conv_b": _row(conv_b[l]),
            "w_gate_lru": jnp.stack(gate_blocks).astype(BF16),
            "b_a": _row(b_a[l]),
            "b_x": _row(b_x[l]),
            "lam": _row(lam[l]),
            "g_out_gmlp": _row(g_out_gmlp[l]),
            "g_out_lru": _row(g_out_lru[l]),
            "w_out": w_out[l],
            "n_post_mix": _row(n_post_mix[l]),
            "n_pre_x": _row(n_pre_x[l]),
            "w_q": w_q[l],
            "kt": kt,
            "v": v,
            "w_o": w_o[l],
            "n_post_x": _row(n_post_x[l]),
            "n_pre_ffn": _row(n_pre_ffn[l]),
            "w_gate": w_gate[l],
            "w_up": w_up[l],
            "w_down": w_down[l],
            "n_post_ffn": _row(n_post_ffn[l]),
        }
        specs = {"kt": kt_spec, "v": v_spec}
        specs.update({k: pl.BlockSpec(memory_space=pl.ANY) for k in _MATMUL_WEIGHTS})
        in_specs = [x_spec] + [specs.get(k) or _resident(params[k].shape) for k in _PARAM_ORDER]
        stage_cols = max(params[k].shape[1] for k in _MATMUL_WEIGHTS)
        assert all(params[k].shape[0] % LOAD_ROWS == 0 for k in _MATMUL_WEIGHTS)

        def layer_body(*refs):
            _layer_kernel(tiles_per_seq, *refs)

        h = pl.pallas_call(
            layer_body,
            grid=(n_tiles + PIPELINE_DEPTH - 1,),
            in_specs=in_specs,
            out_specs=o_spec,
            out_shape=jax.ShapeDtypeStruct((batch, seq, D_MODEL), F32),
            scratch_shapes=[pltpu.VMEM(params[k].shape, BF16) for k in _MATMUL_WEIGHTS] + [
                pltpu.VMEM((LOAD_SLOTS, LOAD_ROWS, stage_cols), F32),
                pltpu.SemaphoreType.DMA((LOAD_SLOTS,)),
                pltpu.VMEM((t, D_MODEL), F32),
                pltpu.VMEM((t, D_MODEL), BF16),
                pltpu.VMEM((t, D_FF), BF16),
                pltpu.VMEM((1, R_WIDTH), F32),
                pltpu.VMEM((t + SUBLANES, R_WIDTH), F32),
                pltpu.VMEM((t // SUBLANES, SUBLANES, R_WIDTH), F32),
                pltpu.VMEM((t // SUBLANES, SUBLANES, R_WIDTH), F32),
                pltpu.VMEM((t // SUBLANES, SUBLANES, R_WIDTH), F32),
            ],
            compiler_params=pltpu.CompilerParams(dimension_semantics=("arbitrary",),
                                                 vmem_limit_bytes=VMEM_LIMIT_BYTES),
            name="layer",
        )(h, *[params[k] for k in _PARAM_ORDER])
    return h
```

```python
import math

import jax
import jax.numpy as jnp
from jax import lax
from jax.experimental import pallas as pl
from jax.experimental.pallas import tpu as pltpu

D_MODEL = 1024
G_WIDTH = 512
R_WIDTH = 512
G_HEADS = 4
G_HEAD_DIM = 128
CHUNK = 128
R_HEADS = 8
R_HEAD_DIM = 64
CONV_W = 4
RG_C = 8.0
N_MEM = 256
X_HEADS = 4
X_HEAD_DIM = 256
D_FF = 2816
EPS = 1e-6

SUBLANES = 8
NEG_LOG2E = -math.log2(math.e)
GATE_BLOCK = 256
TOKENS = 512
FF_CHUNK = 256
DOWN_ROW_BLOCKS = 2
DOWN_COL_BLOCKS = 2
OUT_ROW_BLOCKS = 2
PIPELINE_DEPTH = 2
VMEM_LIMIT_BYTES = 60 * 1024 * 1024

BF16 = jnp.bfloat16
F32 = jnp.float32


def _rms(x, g):
    return x * lax.rsqrt(jnp.mean(x * x, axis=-1, keepdims=True) + EPS) * g


def _sigmoid(x):
    return 1.0 / (1.0 + jnp.exp2(x * NEG_LOG2E))


def _gelu(x):
    k1 = 2.0 * math.sqrt(2.0 / math.pi) * NEG_LOG2E
    return x / (1.0 + jnp.exp2(x * (k1 + (k1 * 0.044715) * (x * x))))


def _dot(a, b):
    return jnp.dot(a, b, preferred_element_type=F32)


def _kv_kernel(mem_ref, n_mem_ref, w_kv_ref, w_q_ref, w_o_ref, w_qk_ref, vw_ref):
    m = _rms(mem_ref[...], n_mem_ref[...]).astype(BF16)
    width = X_HEADS * X_HEAD_DIM
    k = _dot(m, w_kv_ref[:, :width].astype(BF16))
    v = _dot(m, w_kv_ref[:, width:].astype(BF16))
    for hd in range(X_HEADS):
        dims = slice(hd * X_HEAD_DIM, (hd + 1) * X_HEAD_DIM)
        mems = slice(hd * N_MEM, (hd + 1) * N_MEM)
        kt = k[:, dims].T.astype(BF16)
        w_qk = _dot(w_q_ref[:, dims].astype(BF16), kt) * (X_HEAD_DIM ** -0.5)
        w_qk_ref[:, mems] = w_qk.astype(BF16)
        vw = _dot(v[:, dims].astype(BF16), w_o_ref[dims, :].astype(BF16))
        vw_ref[mems, :] = vw.astype(BF16)


def _linear_scan(a, b, h0, a_scr, b_scr, h_scr):
    t, c = a.shape
    groups = t // SUBLANES
    a3 = a.reshape(groups, SUBLANES, c)
    b3 = b.reshape(groups, SUBLANES, c)
    row = lax.broadcasted_iota(jnp.int32, a3.shape, 1)
    shift = 1
    while shift < SUBLANES:
        keep = row >= shift
        a_prev = jnp.where(keep, pltpu.roll(a3, shift, 1), 1.0)
        b_prev = jnp.where(keep, pltpu.roll(b3, shift, 1), 0.0)
        b3 = a3 * b_prev + b3
        a3 = a3 * a_prev
        shift *= 2
    a_scr[...] = a3
    b_scr[...] = b3
    carry = h0
    for g in range(groups):
        blk = a_scr[g] * carry + b_scr[g]
        h_scr[g] = blk
        carry = blk[SUBLANES - 1:SUBLANES, :]
    return h_scr[...].reshape(t, c), carry


def _exchange(blocks):
    sub = lax.broadcasted_iota(jnp.int32, blocks[0].shape, 0)
    v = list(blocks)
    d = SUBLANES // 2
    while d >= 1:
        low = (sub & d) == 0
        nxt = list(v)
        for k in range(SUBLANES):
            if k & d == 0:
                nxt[k] = jnp.where(low, v[k], pltpu.roll(v[k + d], d, 0))
                nxt[k + d] = jnp.where(low, pltpu.roll(v[k], SUBLANES - d, 0), v[k + d])
        v = nxt
        d //= 2
    return v


def _to_segment_layout(x):
    seg_len = x.shape[0] // SUBLANES
    out = []
    for m in range(seg_len // SUBLANES):
        rows = [x[k * seg_len + m * SUBLANES:k * seg_len + (m + 1) * SUBLANES, :]
                for k in range(SUBLANES)]
        out.extend(_exchange(rows))
    return out


def _from_segment_layout(xs):
    seg_len = len(xs)
    groups = [_exchange(xs[m * SUBLANES:(m + 1) * SUBLANES]) for m in range(seg_len // SUBLANES)]
    return jnp.concatenate([groups[m][k] for k in range(SUBLANES)
                            for m in range(seg_len // SUBLANES)], axis=0)


def _stage_a(x_ref, p, h_slot, hn_buf, h_carry, xr_scr, a_scr, b_scr, h_scr):
    yield
    x = x_ref[...]
    t = x.shape[0]
    w_in_ref = p["w_in"]
    hn = _rms(x, p["n_pre_mix"][...]).astype(BF16)

    yield
    u = _gelu(_dot(hn, w_in_ref[:, 0:G_WIDTH]))
    yield
    gv = _gelu(_dot(hn, w_in_ref[:, G_WIDTH:2 * G_WIDTH]))
    mu = jnp.mean(gv, axis=-1, keepdims=True)
    cen = gv - mu
    var = jnp.mean(cen * cen, axis=-1, keepdims=True)
    v = (cen * lax.rsqrt(var + EPS) * p["ln_v_g"][...] + p["ln_v_b"][...]).astype(BF16)
    yield
    tri_r = lax.broadcasted_iota(jnp.int32, (CHUNK, CHUNK), 0)
    tri_c = lax.broadcasted_iota(jnp.int32, (CHUNK, CHUNK), 1)
    causal = tri_r >= tri_c
    ws = [jnp.where(causal, p["w_s"][hd], jnp.zeros((CHUNK, CHUNK), BF16)) for hd in range(G_HEADS)]
    bs = p["b_s"][...]
    rows = []
    for c in range(t // CHUNK):
        cols = []
        for hd in range(G_HEADS):
            vb = v[c * CHUNK:(c + 1) * CHUNK, hd * G_HEAD_DIM:(hd + 1) * G_HEAD_DIM]
            cols.append(_dot(ws[hd], vb))
        rows.append(jnp.concatenate(cols, axis=1) + bs)
    s = jnp.concatenate(rows, axis=0)
    y_g = _rms(u * s, p["g_out_gmlp"][...]).astype(BF16)

    yield
    xr = _dot(hn, w_in_ref[:, 2 * G_WIDTH:2 * G_WIDTH + R_WIDTH])
    xr_scr[SUBLANES:SUBLANES + t, :] = xr
    cw = p["conv_w"][...]
    xc = p["conv_b"][...]
    for k in range(CONV_W):
        off = SUBLANES - (CONV_W - 1) + k
        xc = xc + xr_scr[off:off + t, :] * cw[k:k + 1, :]
    xr_scr[0:SUBLANES, :] = xr[t - SUBLANES:t, :]
    xc_b = xc.astype(BF16)
    yield
    ga, gx = [], []
    for blk in range(R_WIDTH // GATE_BLOCK):
        g2 = _dot(xc_b[:, blk * GATE_BLOCK:(blk + 1) * GATE_BLOCK], p["w_gate_lru"][blk])
        ga.append(g2[:, :GATE_BLOCK])
        gx.append(g2[:, GATE_BLOCK:])
    r = _sigmoid(jnp.concatenate(ga, axis=1) + p["b_a"][...])
    i = _sigmoid(jnp.concatenate(gx, axis=1) + p["b_x"][...])
    nl = -p["lam"][...]
    softplus = jnp.maximum(nl, 0.0) + jnp.log1p(jnp.exp(-jnp.abs(nl)))
    neg_log_a = r * (RG_C * softplus)
    a = jnp.exp2(neg_log_a * NEG_LOG2E)
    mult = jnp.sqrt(jnp.tanh(neg_log_a) * (a * a + 1.0))
    bterm = mult * (i * xc)
    yield
    hseq, h_last = _linear_scan(a, bterm, h_carry[...], a_scr, b_scr, h_scr)
    h_carry[...] = h_last
    yield
    gr = _dot(hn, w_in_ref[:, 2 * G_WIDTH + R_WIDTH:])
    y_r = _rms(_gelu(gr) * hseq, p["g_out_lru"][...]).astype(BF16)
    yield
    y = _dot(jnp.concatenate([y_g, y_r], axis=1), p["w_out"][...])
    h1 = x + _rms(y, p["n_post_mix"][...])

    hn = None
    probs = []
    for hd in range(X_HEADS):
        if hd % 2 == 0:
            yield
            if hn is None:
                hn = _rms(h1, p["n_pre_x"][...]).astype(BF16)
            pair = slice(hd * N_MEM, (hd + 2) * N_MEM)
            sc2 = _dot(hn, p["w_qk"][:, pair])
        sc = sc2[:, (hd % 2) * N_MEM:(hd % 2 + 1) * N_MEM]
        m = jnp.max(sc, axis=-1, keepdims=True)
        e = jnp.exp(sc - m)
        probs.append((e / jnp.sum(e, axis=-1, keepdims=True)).astype(BF16))
    o_cat = jnp.concatenate(probs, axis=1)
    rows = t // OUT_ROW_BLOCKS
    h2s, hn2s = [], []
    for r in range(OUT_ROW_BLOCKS):
        yield
        rs = slice(r * rows, (r + 1) * rows)
        y = _dot(o_cat[rs, :], p["vw"][...])
        h2s.append(h1[rs, :] + _rms(y, p["n_post_x"][...]))
        hn2s.append(_rms(h2s[-1], p["n_pre_ffn"][...]).astype(BF16))
    yield _AFTER_OTHERS
    h_slot[...] = jnp.concatenate(h2s, axis=0)
    hn_buf[...] = jnp.concatenate(hn2s, axis=0)


STAGE_A_COST = (
    (0, 800),
    (1024, 500),
    (1024, 1000),
    (500, 600),
    (1024, 700),
    (512, 1800),
    (0, 800),
    (1024, 1100),
    (2048, 900),
    (1024, 1300),
    (1024, 700),
) + ((2048 // OUT_ROW_BLOCKS, 1700 // OUT_ROW_BLOCKS),) * OUT_ROW_BLOCKS


def _stage_b(h_buf, hn_buf, p, act_scr, o_ref):
    for c in range(D_FF // FF_CHUNK):
        yield
        sl = slice(c * FF_CHUNK, (c + 1) * FF_CHUNK)
        g = _dot(hn_buf[...], p["w_gate"][:, sl])
        up = _dot(hn_buf[...], p["w_up"][:, sl])
        act_scr[:, sl] = (g * _sigmoid(g) * up).astype(BF16)
    rows = act_scr.shape[0] // DOWN_ROW_BLOCKS
    cols = D_MODEL // DOWN_COL_BLOCKS
    for r in range(DOWN_ROW_BLOCKS):
        rs = slice(r * rows, (r + 1) * rows)
        ys = []
        for c in range(DOWN_COL_BLOCKS):
            yield
            ys.append(_dot(act_scr[rs, :], p["w_down"][:, c * cols:(c + 1) * cols]))
        y = jnp.concatenate(ys, axis=1)
        o_ref[rs, :] = h_buf[rs, :] + _rms(y, p["n_post_ffn"][...])


_DOWN_PIECE = (5632 // (DOWN_ROW_BLOCKS * DOWN_COL_BLOCKS), 900 // (DOWN_ROW_BLOCKS * DOWN_COL_BLOCKS))
STAGE_B_COST = (((1024, 250),) * (D_FF // FF_CHUNK)
                + (_DOWN_PIECE,) * (DOWN_ROW_BLOCKS * DOWN_COL_BLOCKS))

_AFTER_OTHERS = object()


def _interleave(lead, lead_cost, fill, fill_cost):
    fill_total = sum(m for m, _ in fill_cost)
    valu_total = sum(v for _, v in lead_cost)
    next(lead)
    next(fill)
    fill_i, fill_done, valu_done, marker = 0, 0.0, 0.0, None
    for k in range(len(lead_cost)):
        marker = next(lead)
        valu_done += lead_cost[k][1]
        target = fill_total * valu_done / valu_total
        while fill_i < len(fill_cost) and fill_done + fill_cost[fill_i][0] / 2 <= target:
            next(fill, None)
            fill_done += fill_cost[fill_i][0]
            fill_i += 1
    if marker is not _AFTER_OTHERS:
        raise ValueError("lead_cost does not match the lead generator's phases")
    for _ in range(fill_i, len(fill_cost)):
        next(fill, None)
    leftovers = [next(fill, None), next(lead, None)]
    if leftovers != [None, None]:
        raise ValueError("phase generators yielded more phases than their cost tables list")


_PARAM_ORDER = (
    "n_pre_mix", "w_in", "ln_v_g", "ln_v_b", "w_s", "b_s", "conv_w", "conv_b", "w_gate_lru",
    "b_a", "b_x", "lam", "g_out_gmlp", "g_out_lru", "w_out", "n_post_mix",
    "n_pre_x", "w_qk", "vw", "n_post_x",
    "n_pre_ffn", "w_gate", "w_up", "w_down", "n_post_ffn",
)
_MATMUL_WEIGHTS = ("w_in", "w_out", "w_gate", "w_up", "w_down")
LOAD_ROWS = 128
LOAD_SLOTS = 4


def _load_weights(hbm_refs, vmem_refs, stage, sems):
    jobs = [(src, dst, r0) for src, dst in zip(hbm_refs, vmem_refs)
            for r0 in range(0, src.shape[0], LOAD_ROWS)]

    def copy(n):
        src, _, r0 = jobs[n]
        slot = n % LOAD_SLOTS
        return pltpu.make_async_copy(src.at[pl.ds(r0, LOAD_ROWS), :],
                                     stage.at[slot, :, pl.ds(0, src.shape[1])], sems.at[slot])

    ahead = LOAD_SLOTS - 1
    for n in range(min(ahead, len(jobs))):
        copy(n).start()
    for n, (src, dst, r0) in enumerate(jobs):
        if n + ahead < len(jobs):
            copy(n + ahead).start()
        copy(n).wait()
        dst[pl.ds(r0, LOAD_ROWS), :] = stage[n % LOAD_SLOTS, :, 0:src.shape[1]].astype(BF16)


def _layer_kernel(tiles_per_seq, x_ref, *rest):
    n_p, n_w = len(_PARAM_ORDER), len(_MATMUL_WEIGHTS)
    p = dict(zip(_PARAM_ORDER, rest[:n_p]))
    o_ref = rest[n_p]
    scratch = rest[n_p + 1:]
    weights, (w_stage, w_sems) = scratch[:n_w], scratch[n_w:n_w + 2]
    h_buf, hn_buf, act_scr, h_carry, xr_scr, a_scr, b_scr, h_scr = scratch[n_w + 2:]
    step = pl.program_id(0)
    n_tiles = pl.num_programs(0) - (PIPELINE_DEPTH - 1)
    tile = jnp.minimum(step, n_tiles - 1)

    @pl.when(step == 0)
    def _():
        _load_weights([p[k] for k in _MATMUL_WEIGHTS], weights, w_stage, w_sems)
        h_buf[...] = jnp.zeros_like(h_buf)
        hn_buf[...] = jnp.zeros_like(hn_buf)

    p.update(zip(_MATMUL_WEIGHTS, weights))

    @pl.when(lax.rem(tile, tiles_per_seq) == 0)
    def _():
        xr_scr[0:SUBLANES, :] = jnp.zeros((SUBLANES, R_WIDTH), F32)
        h_carry[...] = jnp.zeros_like(h_carry)

    stage_a = _stage_a(x_ref, p, h_buf, hn_buf, h_carry, xr_scr, a_scr, b_scr, h_scr)
    stage_b = _stage_b(h_buf, hn_buf, p, act_scr, o_ref)
    _interleave(stage_a, STAGE_A_COST, stage_b, STAGE_B_COST)


def _resident(shape):
    zeros = (0,) * len(shape)
    return pl.BlockSpec(shape, lambda *_: zeros, pipeline_mode=pl.Buffered(1))


def _block_diag(w):
    h, d, _ = w.shape
    eye = jnp.eye(h, dtype=w.dtype)
    return jnp.einsum('hij,hk->hikj', w, eye).reshape(h * d, h * d)


def _row(p):
    return p.reshape(1, -1).astype(F32)


def kernel(x, mem, w_in, ln_v_g, ln_v_b, w_s, b_s, conv_w, conv_b, w_a, b_a, w_x, b_x, lam,
           g_out_gmlp, g_out_lru, w_out, w_q, w_kv, w_o, w_gate, w_up, w_down, n_pre_mix,
           n_post_mix, n_pre_x, n_mem, n_post_x, n_pre_ffn, n_post_ffn):
    batch, seq, _ = x.shape
    depth = w_in.shape[0]
    t = TOKENS
    assert seq % t == 0 and t % CHUNK == 0
    tiles_per_seq = seq // t
    n_tiles = batch * tiles_per_seq
    heads_per_block = GATE_BLOCK // R_HEAD_DIM

    def tile_of_a(i):
        return jnp.minimum(i, n_tiles - 1)

    def tile_of_b(i):
        return jnp.maximum(i - (PIPELINE_DEPTH - 1), 0)

    x_spec = pl.BlockSpec((None, t, D_MODEL),
                          lambda i: (tile_of_a(i) // tiles_per_seq, tile_of_a(i) % tiles_per_seq, 0))
    o_spec = pl.BlockSpec((None, t, D_MODEL),
                          lambda i: (tile_of_b(i) // tiles_per_seq, tile_of_b(i) % tiles_per_seq, 0))
    att_width = X_HEADS * N_MEM
    w_qk_spec = pl.BlockSpec((None, D_MODEL, att_width), lambda i: (tile_of_a(i) // tiles_per_seq, 0, 0),
                             pipeline_mode=pl.Buffered(1))
    vw_spec = pl.BlockSpec((None, att_width, D_MODEL), lambda i: (tile_of_a(i) // tiles_per_seq, 0, 0),
                           pipeline_mode=pl.Buffered(1))

    h = x
    for l in range(depth):
        kv_in = [mem, _row(n_mem[l]), w_kv[l], w_q[l], w_o[l]]
        w_qk, vw = pl.pallas_call(
            _kv_kernel,
            grid=(batch,),
            in_specs=[pl.BlockSpec((None, N_MEM, D_MODEL), lambda b: (b, 0, 0))]
                     + [_resident(a.shape) for a in kv_in[1:]],
            out_specs=[pl.BlockSpec((None, D_MODEL, att_width), lambda b: (b, 0, 0)),
                       pl.BlockSpec((None, att_width, D_MODEL), lambda b: (b, 0, 0))],
            out_shape=[jax.ShapeDtypeStruct((batch, D_MODEL, att_width), BF16),
                       jax.ShapeDtypeStruct((batch, att_width, D_MODEL), BF16)],
            compiler_params=pltpu.CompilerParams(dimension_semantics=("arbitrary",),
                                                 vmem_limit_bytes=VMEM_LIMIT_BYTES),
            name="kv_proj",
        )(*kv_in)

        gate_blocks = []
        for blk in range(R_WIDTH // GATE_BLOCK):
            hs = slice(blk * heads_per_block, (blk + 1) * heads_per_block)
            gate_blocks.append(jnp.concatenate(
                [_block_diag(w_a[l, hs]), _block_diag(w_x[l, hs])], axis=1))

        params = {
            "n_pre_mix": _row(n_pre_mix[l]),
            "w_in": w_in[l],
            "ln_v_g": _row(ln_v_g[l]),
            "ln_v_b": _row(ln_v_b[l]),
            "w_s": w_s[l].astype(BF16),
            "b_s": jnp.repeat(b_s[l].T, G_HEAD_DIM, axis=1).astype(F32),
            "conv_w": conv_w[l].astype(F32),
            "conv_b": _row(conv_b[l]),
            "w_gate_lru": jnp.stack(gate_blocks).astype(BF16),
            "b_a": _row(b_a[l]),
            "b_x": _row(b_x[l]),
            "lam": _row(lam[l]),
            "g_out_gmlp": _row(g_out_gmlp[l]),
            "g_out_lru": _row(g_out_lru[l]),
            "w_out": w_out[l],
            "n_post_mix": _row(n_post_mix[l]),
            "n_pre_x": _row(n_pre_x[l]),
            "w_qk": w_qk,
            "vw": vw,
            "n_post_x": _row(n_post_x[l]),
            "n_pre_ffn": _row(n_pre_ffn[l]),
            "w_gate": w_gate[l],
            "w_up": w_up[l],
            "w_down": w_down[l],
            "n_post_ffn": _row(n_post_ffn[l]),
        }
        specs = {"w_qk": w_qk_spec, "vw": vw_spec}
        specs.update({k: pl.BlockSpec(memory_space=pl.ANY) for k in _MATMUL_WEIGHTS})
        in_specs = [x_spec] + [specs.get(k) or _resident(params[k].shape) for k in _PARAM_ORDER]
        stage_cols = max(params[k].shape[1] for k in _MATMUL_WEIGHTS)
        assert all(params[k].shape[0] % LOAD_ROWS == 0 for k in _MATMUL_WEIGHTS)

        def layer_body(*refs):
            _layer_kernel(tiles_per_seq, *refs)

        h = pl.pallas_call(
            layer_body,
            grid=(n_tiles + PIPELINE_DEPTH - 1,),
            in_specs=in_specs,
            out_specs=o_spec,
            out_shape=jax.ShapeDtypeStruct((batch, seq, D_MODEL), F32),
            scratch_shapes=[pltpu.VMEM(params[k].shape, BF16) for k in _MATMUL_WEIGHTS] + [
                pltpu.VMEM((LOAD_SLOTS, LOAD_ROWS, stage_cols), F32),
                pltpu.SemaphoreType.DMA((LOAD_SLOTS,)),
                pltpu.VMEM((t, D_MODEL), F32),
                pltpu.VMEM((t, D_MODEL), BF16),
                pltpu.VMEM((t, D_FF), BF16),
                pltpu.VMEM((1, R_WIDTH), F32),
                pltpu.VMEM((t + SUBLANES, R_WIDTH), F32),
                pltpu.VMEM((t // SUBLANES, SUBLANES, R_WIDTH), F32),
                pltpu.VMEM((t // SUBLANES, SUBLANES, R_WIDTH), F32),
                pltpu.VMEM((t // SUBLANES, SUBLANES, R_WIDTH), F32),
            ],
            compiler_params=pltpu.CompilerParams(dimension_semantics=("arbitrary",),
                                                 vmem_limit_bytes=VMEM_LIMIT_BYTES),
            name="layer",
        )(h, *[params[k] for k in _PARAM_ORDER])
    return h
```

```python
import math

import jax
import jax.numpy as jnp
from jax import lax
from jax.experimental import pallas as pl
from jax.experimental.pallas import tpu as pltpu

D_MODEL = 1024
G_WIDTH = 512
R_WIDTH = 512
G_HEADS = 4
G_HEAD_DIM = 128
CHUNK = 128
R_HEADS = 8
R_HEAD_DIM = 64
CONV_W = 4
RG_C = 8.0
N_MEM = 256
X_HEADS = 4
X_HEAD_DIM = 256
D_FF = 2816
EPS = 1e-6

SUBLANES = 8
NEG_LOG2E = -math.log2(math.e)
GATE_BLOCK = 256
TOKENS = 512
FF_CHUNK = 256
DOWN_ROW_BLOCKS = 2
DOWN_COL_BLOCKS = 2
OUT_ROW_BLOCKS = 2
PIPELINE_DEPTH = 2
VMEM_LIMIT_BYTES = 60 * 1024 * 1024

BF16 = jnp.bfloat16
F32 = jnp.float32


def _rms(x, g):
    return x * lax.rsqrt(jnp.mean(x * x, axis=-1, keepdims=True) + EPS) * g


def _sigmoid(x):
    return 1.0 / (1.0 + jnp.exp2(x * NEG_LOG2E))


def _gelu(x):
    k1 = 2.0 * math.sqrt(2.0 / math.pi) * NEG_LOG2E
    return x / (1.0 + jnp.exp2(x * (k1 + (k1 * 0.044715) * (x * x))))


def _dot(a, b):
    return jnp.dot(a, b, preferred_element_type=F32)


def _kv_kernel(mem_ref, n_mem_ref, w_kv_ref, w_q_ref, w_o_ref, w_qk_ref, vw_ref):
    m = _rms(mem_ref[...], n_mem_ref[...]).astype(BF16)
    width = X_HEADS * X_HEAD_DIM
    k = _dot(m, w_kv_ref[:, :width].astype(BF16))
    v = _dot(m, w_kv_ref[:, width:].astype(BF16))
    for hd in range(X_HEADS):
        dims = slice(hd * X_HEAD_DIM, (hd + 1) * X_HEAD_DIM)
        mems = slice(hd * N_MEM, (hd + 1) * N_MEM)
        kt = k[:, dims].T.astype(BF16)
        w_qk = _dot(w_q_ref[:, dims].astype(BF16), kt) * (X_HEAD_DIM ** -0.5)
        w_qk_ref[:, mems] = w_qk.astype(BF16)
        vw = _dot(v[:, dims].astype(BF16), w_o_ref[dims, :].astype(BF16))
        vw_ref[mems, :] = vw.astype(BF16)


def _linear_scan(a, b, h0, a_scr, b_scr, h_scr):
    t, c = a.shape
    groups = t // SUBLANES
    a3 = a.reshape(groups, SUBLANES, c)
    b3 = b.reshape(groups, SUBLANES, c)
    row = lax.broadcasted_iota(jnp.int32, a3.shape, 1)
    shift = 1
    while shift < SUBLANES:
        keep = row >= shift
        a_prev = jnp.where(keep, pltpu.roll(a3, shift, 1), 1.0)
        b_prev = jnp.where(keep, pltpu.roll(b3, shift, 1), 0.0)
        b3 = a3 * b_prev + b3
        a3 = a3 * a_prev
        shift *= 2
    a_scr[...] = a3
    b_scr[...] = b3
    carry = h0
    for g in range(groups):
        blk = a_scr[g] * carry + b_scr[g]
        h_scr[g] = blk
        carry = blk[SUBLANES - 1:SUBLANES, :]
    return h_scr[...].reshape(t, c), carry


def _exchange(blocks):
    sub = lax.broadcasted_iota(jnp.int32, blocks[0].shape, 0)
    v = list(blocks)
    d = SUBLANES // 2
    while d >= 1:
        low = (sub & d) == 0
        nxt = list(v)
        for k in range(SUBLANES):
            if k & d == 0:
                nxt[k] = jnp.where(low, v[k], pltpu.roll(v[k + d], d, 0))
                nxt[k + d] = jnp.where(low, pltpu.roll(v[k], SUBLANES - d, 0), v[k + d])
        v = nxt
        d //= 2
    return v


def _to_segment_layout(x):
    seg_len = x.shape[0] // SUBLANES
    out = []
    for m in range(seg_len // SUBLANES):
        rows = [x[k * seg_len + m * SUBLANES:k * seg_len + (m + 1) * SUBLANES, :]
                for k in range(SUBLANES)]
        out.extend(_exchange(rows))
    return out


def _from_segment_layout(xs):
    seg_len = len(xs)
    groups = [_exchange(xs[m * SUBLANES:(m + 1) * SUBLANES]) for m in range(seg_len // SUBLANES)]
    return jnp.concatenate([groups[m][k] for k in range(SUBLANES)
                            for m in range(seg_len // SUBLANES)], axis=0)


def _stage_a(x_ref, p, h_slot, hn_buf, h_carry, xr_scr, a_scr, b_scr, h_scr):
    yield
    x = x_ref[...]
    t = x.shape[0]
    w_in_ref = p["w_in"]
    hn = _rms(x, p["n_pre_mix"][...]).astype(BF16)

    yield
    u = _gelu(_dot(hn, w_in_ref[:, 0:G_WIDTH]))
    yield
    gv = _gelu(_dot(hn, w_in_ref[:, G_WIDTH:2 * G_WIDTH]))
    mu = jnp.mean(gv, axis=-1, keepdims=True)
    cen = gv - mu
    var = jnp.mean(cen * cen, axis=-1, keepdims=True)
    v = (cen * lax.rsqrt(var + EPS) * p["ln_v_g"][...] + p["ln_v_b"][...]).astype(BF16)
    yield
    tri_r = lax.broadcasted_iota(jnp.int32, (CHUNK, CHUNK), 0)
    tri_c = lax.broadcasted_iota(jnp.int32, (CHUNK, CHUNK), 1)
    causal = tri_r >= tri_c
    ws = [jnp.where(causal, p["w_s"][hd], jnp.zeros((CHUNK, CHUNK), BF16)) for hd in range(G_HEADS)]
    bs = p["b_s"][...]
    rows = []
    for c in range(t // CHUNK):
        cols = []
        for hd in range(G_HEADS):
            vb = v[c * CHUNK:(c + 1) * CHUNK, hd * G_HEAD_DIM:(hd + 1) * G_HEAD_DIM]
            cols.append(_dot(ws[hd], vb))
        rows.append(jnp.concatenate(cols, axis=1) + bs)
    s = jnp.concatenate(rows, axis=0)
    y_g = _rms(u * s, p["g_out_gmlp"][...]).astype(BF16)

    yield
    xr = _dot(hn, w_in_ref[:, 2 * G_WIDTH:2 * G_WIDTH + R_WIDTH])
    xr_scr[SUBLANES:SUBLANES + t, :] = xr
    cw = p["conv_w"][...]
    xc = p["conv_b"][...]
    for k in range(CONV_W):
        off = SUBLANES - (CONV_W - 1) + k
        xc = xc + xr_scr[off:off + t, :] * cw[k:k + 1, :]
    xr_scr[0:SUBLANES, :] = xr[t - SUBLANES:t, :]
    xc_b = xc.astype(BF16)
    yield
    ga, gx = [], []
    for blk in range(R_WIDTH // GATE_BLOCK):
        g2 = _dot(xc_b[:, blk * GATE_BLOCK:(blk + 1) * GATE_BLOCK], p["w_gate_lru"][blk])
        ga.append(g2[:, :GATE_BLOCK])
        gx.append(g2[:, GATE_BLOCK:])
    r = _sigmoid(jnp.concatenate(ga, axis=1) + p["b_a"][...])
    i = _sigmoid(jnp.concatenate(gx, axis=1) + p["b_x"][...])
    nl = -p["lam"][...]
    softplus = jnp.maximum(nl, 0.0) + jnp.log1p(jnp.exp(-jnp.abs(nl)))
    neg_log_a = r * (RG_C * softplus)
    a = jnp.exp2(neg_log_a * NEG_LOG2E)
    mult = jnp.sqrt(jnp.tanh(neg_log_a) * (a * a + 1.0))
    bterm = mult * (i * xc)
    yield
    hseq, h_last = _linear_scan(a, bterm, h_carry[...], a_scr, b_scr, h_scr)
    h_carry[...] = h_last
    yield
    gr = _dot(hn, w_in_ref[:, 2 * G_WIDTH + R_WIDTH:])
    y_r = _rms(_gelu(gr) * hseq, p["g_out_lru"][...]).astype(BF16)
    yield
    y = _dot(jnp.concatenate([y_g, y_r], axis=1), p["w_out"][...])
    h1 = x + _rms(y, p["n_post_mix"][...])

    hn = None
    probs = []
    for hd in range(X_HEADS):
        if hd % 2 == 0:
            yield
            if hn is None:
                hn = _rms(h1, p["n_pre_x"][...]).astype(BF16)
            pair = slice(hd * N_MEM, (hd + 2) * N_MEM)
            sc2 = _dot(hn, p["w_qk"][:, pair])
        sc = sc2[:, (hd % 2) * N_MEM:(hd % 2 + 1) * N_MEM]
        m = jnp.max(sc, axis=-1, keepdims=True)
        e = jnp.exp(sc - m)
        probs.append((e / jnp.sum(e, axis=-1, keepdims=True)).astype(BF16))
    o_cat = jnp.concatenate(probs, axis=1)
    rows = t // OUT_ROW_BLOCKS
    h2s, hn2s = [], []
    for r in range(OUT_ROW_BLOCKS):
        yield
        rs = slice(r * rows, (r + 1) * rows)
        y = _dot(o_cat[rs, :], p["vw"][...])
        h2s.append(h1[rs, :] + _rms(y, p["n_post_x"][...]))
        hn2s.append(_rms(h2s[-1], p["n_pre_ffn"][...]).astype(BF16))
    yield _AFTER_OTHERS
    h_slot[...] = jnp.concatenate(h2s, axis=0)
    hn_buf[...] = jnp.concatenate(hn2s, axis=0)


STAGE_A_COST = (
    (0, 800),
    (1024, 500),
    (1024, 1000),
    (500, 600),
    (1024, 700),
    (512, 1800),
    (0, 800),
    (1024, 1100),
    (2048, 900),
    (1024, 1300),
    (1024, 700),
) + ((2048 // OUT_ROW_BLOCKS, 1700 // OUT_ROW_BLOCKS),) * OUT_ROW_BLOCKS


def _stage_b(h_buf, hn_buf, p, act_scr, o_ref):
    for c in range(D_FF // FF_CHUNK):
        yield
        sl = slice(c * FF_CHUNK, (c + 1) * FF_CHUNK)
        g = _dot(hn_buf[...], p["w_gate"][:, sl])
        up = _dot(hn_buf[...], p["w_up"][:, sl])
        act_scr[:, sl] = (g * _sigmoid(g) * up).astype(BF16)
    rows = act_scr.shape[0] // DOWN_ROW_BLOCKS
    cols = D_MODEL // DOWN_COL_BLOCKS
    for r in range(DOWN_ROW_BLOCKS):
        rs = slice(r * rows, (r + 1) * rows)
        ys = []
        for c in range(DOWN_COL_BLOCKS):
            yield
            ys.append(_dot(act_scr[rs, :], p["w_down"][:, c * cols:(c + 1) * cols]))
        y = jnp.concatenate(ys, axis=1)
        o_ref[rs, :] = h_buf[rs, :] + _rms(y, p["n_post_ffn"][...])


_DOWN_PIECE = (5632 // (DOWN_ROW_BLOCKS * DOWN_COL_BLOCKS), 900 // (DOWN_ROW_BLOCKS * DOWN_COL_BLOCKS))
STAGE_B_COST = (((1024, 250),) * (D_FF // FF_CHUNK)
                + (_DOWN_PIECE,) * (DOWN_ROW_BLOCKS * DOWN_COL_BLOCKS))

_AFTER_OTHERS = object()


def _interleave(lead, lead_cost, fill, fill_cost):
    fill_total = sum(m for m, _ in fill_cost)
    valu_total = sum(v for _, v in lead_cost)
    next(lead)
    next(fill)
    fill_i, fill_done, valu_done, marker = 0, 0.0, 0.0, None
    for k in range(len(lead_cost)):
        marker = next(lead)
        valu_done += lead_cost[k][1]
        target = fill_total * valu_done / valu_total
        while fill_i < len(fill_cost) and fill_done + fill_cost[fill_i][0] / 2 <= target:
            next(fill, None)
            fill_done += fill_cost[fill_i][0]
            fill_i += 1
    if marker is not _AFTER_OTHERS:
        raise ValueError("lead_cost does not match the lead generator's phases")
    for _ in range(fill_i, len(fill_cost)):
        next(fill, None)
    leftovers = [next(fill, None), next(lead, None)]
    if leftovers != [None, None]:
        raise ValueError("phase generators yielded more phases than their cost tables list")


_PARAM_ORDER = (
    "n_pre_mix", "w_in", "ln_v_g", "ln_v_b", "w_s", "b_s", "conv_w", "conv_b", "w_gate_lru",
    "b_a", "b_x", "lam", "g_out_gmlp", "g_out_lru", "w_out", "n_post_mix",
    "n_pre_x", "w_qk", "vw", "n_post_x",
    "n_pre_ffn", "w_gate", "w_up", "w_down", "n_post_ffn",
)
_MATMUL_WEIGHTS = ("w_in", "w_out", "w_gate", "w_up", "w_down")
LOAD_ROWS = 128
LOAD_SLOTS = 4


def _load_weights(hbm_refs, vmem_refs, stage, sems):
    jobs = [(src, dst, r0) for src, dst in zip(hbm_refs, vmem_refs)
            for r0 in range(0, src.shape[0], LOAD_ROWS)]

    def copy(n):
        src, _, r0 = jobs[n]
        slot = n % LOAD_SLOTS
        return pltpu.make_async_copy(src.at[pl.ds(r0, LOAD_ROWS), :],
                                     stage.at[slot, :, pl.ds(0, src.shape[1])], sems.at[slot])

    ahead = LOAD_SLOTS - 1
    for n in range(min(ahead, len(jobs))):
        copy(n).start()
    for n, (src, dst, r0) in enumerate(jobs):
        if n + ahead < len(jobs):
            copy(n + ahead).start()
        copy(n).wait()
        dst[pl.ds(r0, LOAD_ROWS), :] = stage[n % LOAD_SLOTS, :, 0:src.shape[1]].astype(BF16)


def _layer_kernel(tiles_per_seq, x_ref, *rest):
    n_p, n_w = len(_PARAM_ORDER), len(_MATMUL_WEIGHTS)
    p = dict(zip(_PARAM_ORDER, rest[:n_p]))
    o_ref = rest[n_p]
    scratch = rest[n_p + 1:]
    weights, (w_stage, w_sems) = scratch[:n_w], scratch[n_w:n_w + 2]
    h_buf, hn_buf, act_scr, h_carry, xr_scr, a_scr, b_scr, h_scr = scratch[n_w + 2:]
    step = pl.program_id(0)
    n_tiles = pl.num_programs(0) - (PIPELINE_DEPTH - 1)
    hbm_weights = [p[k] for k in _MATMUL_WEIGHTS]
    p.update(zip(_MATMUL_WEIGHTS, weights))

    def stage_a():
        return _stage_a(x_ref, p, h_buf, hn_buf, h_carry, xr_scr, a_scr, b_scr, h_scr)

    def stage_b():
        return _stage_b(h_buf, hn_buf, p, act_scr, o_ref)

    @pl.when(lax.rem(step, tiles_per_seq) == 0)
    def _():
        xr_scr[0:SUBLANES, :] = jnp.zeros((SUBLANES, R_WIDTH), F32)
        h_carry[...] = jnp.zeros_like(h_carry)

    @pl.when(step == 0)
    def _():
        _load_weights(hbm_weights, weights, w_stage, w_sems)
        for _ in stage_a():
            pass
        o_ref[...] = jnp.zeros_like(o_ref)

    @pl.when(jnp.logical_and(step > 0, step < n_tiles))
    def _():
        _interleave(stage_a(), STAGE_A_COST, stage_b(), STAGE_B_COST)

    @pl.when(step == n_tiles)
    def _():
        for _ in stage_b():
            pass


def _resident(shape):
    zeros = (0,) * len(shape)
    return pl.BlockSpec(shape, lambda *_: zeros, pipeline_mode=pl.Buffered(1))


def _block_diag(w):
    h, d, _ = w.shape
    eye = jnp.eye(h, dtype=w.dtype)
    return jnp.einsum('hij,hk->hikj', w, eye).reshape(h * d, h * d)


def _row(p):
    return p.reshape(1, -1).astype(F32)


def kernel(x, mem, w_in, ln_v_g, ln_v_b, w_s, b_s, conv_w, conv_b, w_a, b_a, w_x, b_x, lam,
           g_out_gmlp, g_out_lru, w_out, w_q, w_kv, w_o, w_gate, w_up, w_down, n_pre_mix,
           n_post_mix, n_pre_x, n_mem, n_post_x, n_pre_ffn, n_post_ffn):
    batch, seq, _ = x.shape
    depth = w_in.shape[0]
    t = TOKENS
    assert seq % t == 0 and t % CHUNK == 0
    tiles_per_seq = seq // t
    n_tiles = batch * tiles_per_seq
    heads_per_block = GATE_BLOCK // R_HEAD_DIM

    def tile_of_a(i):
        return jnp.minimum(i, n_tiles - 1)

    def tile_of_b(i):
        return jnp.maximum(i - (PIPELINE_DEPTH - 1), 0)

    x_spec = pl.BlockSpec((None, t, D_MODEL),
                          lambda i: (tile_of_a(i) // tiles_per_seq, tile_of_a(i) % tiles_per_seq, 0))
    o_spec = pl.BlockSpec((None, t, D_MODEL),
                          lambda i: (tile_of_b(i) // tiles_per_seq, tile_of_b(i) % tiles_per_seq, 0))
    att_width = X_HEADS * N_MEM
    w_qk_spec = pl.BlockSpec((None, D_MODEL, att_width), lambda i: (tile_of_a(i) // tiles_per_seq, 0, 0),
                             pipeline_mode=pl.Buffered(1))
    vw_spec = pl.BlockSpec((None, att_width, D_MODEL), lambda i: (tile_of_a(i) // tiles_per_seq, 0, 0),
                           pipeline_mode=pl.Buffered(1))

    h = x
    for l in range(depth):
        kv_in = [mem, _row(n_mem[l]), w_kv[l], w_q[l], w_o[l]]
        w_qk, vw = pl.pallas_call(
            _kv_kernel,
            grid=(batch,),
            in_specs=[pl.BlockSpec((None, N_MEM, D_MODEL), lambda b: (b, 0, 0))]
                     + [_resident(a.shape) for a in kv_in[1:]],
            out_specs=[pl.BlockSpec((None, D_MODEL, att_width), lambda b: (b, 0, 0)),
                       pl.BlockSpec((None, att_width, D_MODEL), lambda b: (b, 0, 0))],
            out_shape=[jax.ShapeDtypeStruct((batch, D_MODEL, att_width), BF16),
                       jax.ShapeDtypeStruct((batch, att_width, D_MODEL), BF16)],
            compiler_params=pltpu.CompilerParams(dimension_semantics=("arbitrary",),
                                                 vmem_limit_bytes=VMEM_LIMIT_BYTES),
            name="kv_proj",
        )(*kv_in)

        gate_blocks = []
        for blk in range(R_WIDTH // GATE_BLOCK):
            hs = slice(blk * heads_per_block, (blk + 1) * heads_per_block)
            gate_blocks.append(jnp.concatenate(
                [_block_diag(w_a[l, hs]), _block_diag(w_x[l, hs])], axis=1))

        params = {
            "n_pre_mix": _row(n_pre_mix[l]),
            "w_in": w_in[l],
            "ln_v_g": _row(ln_v_g[l]),
            "ln_v_b": _row(ln_v_b[l]),
            "w_s": w_s[l].astype(BF16),
            "b_s": jnp.repeat(b_s[l].T, G_HEAD_DIM, axis=1).astype(F32),
            "conv_w": conv_w[l].astype(F32),
            "conv_b": _row(conv_b[l]),
            "w_gate_lru": jnp.stack(gate_blocks).astype(BF16),
            "b_a": _row(b_a[l]),
            "b_x": _row(b_x[l]),
            "lam": _row(lam[l]),
            "g_out_gmlp": _row(g_out_gmlp[l]),
            "g_out_lru": _row(g_out_lru[l]),
            "w_out": w_out[l],
            "n_post_mix": _row(n_post_mix[l]),
            "n_pre_x": _row(n_pre_x[l]),
            "w_qk": w_qk,
            "vw": vw,
            "n_post_x": _row(n_post_x[l]),
            "n_pre_ffn": _row(n_pre_ffn[l]),
            "w_gate": w_gate[l],
            "w_up": w_up[l],
            "w_down": w_down[l],
            "n_post_ffn": _row(n_post_ffn[l]),
        }
        specs = {"w_qk": w_qk_spec, "vw": vw_spec}
        specs.update({k: pl.BlockSpec(memory_space=pl.ANY) for k in _MATMUL_WEIGHTS})
        in_specs = [x_spec] + [specs.get(k) or _resident(params[k].shape) for k in _PARAM_ORDER]
        stage_cols = max(params[k].shape[1] for k in _MATMUL_WEIGHTS)
        assert all(params[k].shape[0] % LOAD_ROWS == 0 for k in _MATMUL_WEIGHTS)

        def layer_body(*refs):
            _layer_kernel(tiles_per_seq, *refs)

        h = pl.pallas_call(
            layer_body,
            grid=(n_tiles + PIPELINE_DEPTH - 1,),
            in_specs=in_specs,
            out_specs=o_spec,
            out_shape=jax.ShapeDtypeStruct((batch, seq, D_MODEL), F32),
            scratch_shapes=[pltpu.VMEM(params[k].shape, BF16) for k in _MATMUL_WEIGHTS] + [
                pltpu.VMEM((LOAD_SLOTS, LOAD_ROWS, stage_cols), F32),
                pltpu.SemaphoreType.DMA((LOAD_SLOTS,)),
                pltpu.VMEM((t, D_MODEL), F32),
                pltpu.VMEM((t, D_MODEL), BF16),
                pltpu.VMEM((t, D_FF), BF16),
                pltpu.VMEM((1, R_WIDTH), F32),
                pltpu.VMEM((t + SUBLANES, R_WIDTH), F32),
                pltpu.VMEM((t // SUBLANES, SUBLANES, R_WIDTH), F32),
                pltpu.VMEM((t // SUBLANES, SUBLANES, R_WIDTH), F32),
                pltpu.VMEM((t // SUBLANES, SUBLANES, R_WIDTH), F32),
            ],
            compiler_params=pltpu.CompilerParams(dimension_semantics=("arbitrary",),
                                                 vmem_limit_bytes=VMEM_LIMIT_BYTES),
            name="layer",
        )(h, *[params[k] for k in _PARAM_ORDER])
    return h
```

```python
import math

import jax
import jax.numpy as jnp
from jax import lax
from jax.experimental import pallas as pl
from jax.experimental.pallas import tpu as pltpu

D_MODEL = 1024
G_WIDTH = 512
R_WIDTH = 512
G_HEADS = 4
G_HEAD_DIM = 128
CHUNK = 128
R_HEADS = 8
R_HEAD_DIM = 64
CONV_W = 4
RG_C = 8.0
N_MEM = 256
X_HEADS = 4
X_HEAD_DIM = 256
D_FF = 2816
EPS = 1e-6

SUBLANES = 8
NEG_LOG2E = -math.log2(math.e)
GATE_BLOCK = 256
TOKENS = 512
FF_CHUNK = 256
DOWN_ROW_BLOCKS = 2
DOWN_COL_BLOCKS = 2
OUT_ROW_BLOCKS = 2
PIPELINE_DEPTH = 2
VMEM_LIMIT_BYTES = 60 * 1024 * 1024

BF16 = jnp.bfloat16
F32 = jnp.float32


def _rms(x, g):
    return x * lax.rsqrt(jnp.mean(x * x, axis=-1, keepdims=True) + EPS) * g


def _sigmoid(x):
    return 1.0 / (1.0 + jnp.exp2(x * NEG_LOG2E))


def _gelu(x):
    k1 = 2.0 * math.sqrt(2.0 / math.pi) * NEG_LOG2E
    return x / (1.0 + jnp.exp2(x * (k1 + (k1 * 0.044715) * (x * x))))


def _dot(a, b):
    return jnp.dot(a, b, preferred_element_type=F32)


def _kv_kernel(mem_ref, n_mem_ref, w_k_ref, w_v_ref, w_q_ref, w_o_ref, w_qk_ref, vw_ref):
    m = _rms(mem_ref[...], n_mem_ref[...]).astype(BF16)
    k = _dot(m, w_k_ref[...].astype(BF16))
    v = _dot(m, w_v_ref[...].astype(BF16))
    w_qk = _dot(w_q_ref[...].astype(BF16), k.T.astype(BF16)) * (X_HEAD_DIM ** -0.5)
    w_qk_ref[...] = w_qk.astype(BF16)
    vw_ref[...] = _dot(v.astype(BF16), w_o_ref[...].astype(BF16)).astype(BF16)


def _linear_scan(a, b, h0, a_scr, b_scr, h_scr):
    t, c = a.shape
    groups = t // SUBLANES
    a3 = a.reshape(groups, SUBLANES, c)
    b3 = b.reshape(groups, SUBLANES, c)
    row = lax.broadcasted_iota(jnp.int32, a3.shape, 1)
    shift = 1
    while shift < SUBLANES:
        keep = row >= shift
        a_prev = jnp.where(keep, pltpu.roll(a3, shift, 1), 1.0)
        b_prev = jnp.where(keep, pltpu.roll(b3, shift, 1), 0.0)
        b3 = a3 * b_prev + b3
        a3 = a3 * a_prev
        shift *= 2
    a_scr[...] = a3
    b_scr[...] = b3
    carry = h0
    for g in range(groups):
        blk = a_scr[g] * carry + b_scr[g]
        h_scr[g] = blk
        carry = blk[SUBLANES - 1:SUBLANES, :]
    return h_scr[...].reshape(t, c), carry


def _exchange(blocks):
    sub = lax.broadcasted_iota(jnp.int32, blocks[0].shape, 0)
    v = list(blocks)
    d = SUBLANES // 2
    while d >= 1:
        low = (sub & d) == 0
        nxt = list(v)
        for k in range(SUBLANES):
            if k & d == 0:
                nxt[k] = jnp.where(low, v[k], pltpu.roll(v[k + d], d, 0))
                nxt[k + d] = jnp.where(low, pltpu.roll(v[k], SUBLANES - d, 0), v[k + d])
        v = nxt
        d //= 2
    return v


def _to_segment_layout(x):
    seg_len = x.shape[0] // SUBLANES
    out = []
    for m in range(seg_len // SUBLANES):
        rows = [x[k * seg_len + m * SUBLANES:k * seg_len + (m + 1) * SUBLANES, :]
                for k in range(SUBLANES)]
        out.extend(_exchange(rows))
    return out


def _from_segment_layout(xs):
    seg_len = len(xs)
    groups = [_exchange(xs[m * SUBLANES:(m + 1) * SUBLANES]) for m in range(seg_len // SUBLANES)]
    return jnp.concatenate([groups[m][k] for k in range(SUBLANES)
                            for m in range(seg_len // SUBLANES)], axis=0)


def _stage_a(x_ref, p, h_slot, hn_buf, h_carry, xr_scr, a_scr, b_scr, h_scr):
    yield
    x = x_ref[...]
    t = x.shape[0]
    w_in_ref = p["w_in"]
    hn = _rms(x, p["n_pre_mix"][...]).astype(BF16)

    yield
    u = _gelu(_dot(hn, w_in_ref[:, 0:G_WIDTH]))
    yield
    gv = _gelu(_dot(hn, w_in_ref[:, G_WIDTH:2 * G_WIDTH]))
    mu = jnp.mean(gv, axis=-1, keepdims=True)
    cen = gv - mu
    var = jnp.mean(cen * cen, axis=-1, keepdims=True)
    v = (cen * lax.rsqrt(var + EPS) * p["ln_v_g"][...] + p["ln_v_b"][...]).astype(BF16)
    yield
    tri_r = lax.broadcasted_iota(jnp.int32, (CHUNK, CHUNK), 0)
    tri_c = lax.broadcasted_iota(jnp.int32, (CHUNK, CHUNK), 1)
    causal = tri_r >= tri_c
    ws = [jnp.where(causal, p["w_s"][hd], 0.0).astype(BF16) for hd in range(G_HEADS)]
    bs = p["b_s"][...]
    rows = []
    for c in range(t // CHUNK):
        cols = []
        for hd in range(G_HEADS):
            vb = v[c * CHUNK:(c + 1) * CHUNK, hd * G_HEAD_DIM:(hd + 1) * G_HEAD_DIM]
            cols.append(_dot(ws[hd], vb))
        rows.append(jnp.concatenate(cols, axis=1) + bs)
    s = jnp.concatenate(rows, axis=0)
    y_g = _rms(u * s, p["g_out_gmlp"][...]).astype(BF16)

    yield
    xr = _dot(hn, w_in_ref[:, 2 * G_WIDTH:2 * G_WIDTH + R_WIDTH])
    xr_scr[SUBLANES:SUBLANES + t, :] = xr
    cw = p["conv_w"][...]
    xc = p["conv_b"][...]
    for k in range(CONV_W):
        off = SUBLANES - (CONV_W - 1) + k
        xc = xc + xr_scr[off:off + t, :] * cw[k:k + 1, :]
    xr_scr[0:SUBLANES, :] = xr[t - SUBLANES:t, :]
    xc_b = xc.astype(BF16)
    yield
    ga, gx = [], []
    for blk in range(R_WIDTH // GATE_BLOCK):
        g2 = _dot(xc_b[:, blk * GATE_BLOCK:(blk + 1) * GATE_BLOCK], p["w_gate_lru"][blk])
        ga.append(g2[:, :GATE_BLOCK])
        gx.append(g2[:, GATE_BLOCK:])
    r = _sigmoid(jnp.concatenate(ga, axis=1) + p["b_a"][...])
    i = _sigmoid(jnp.concatenate(gx, axis=1) + p["b_x"][...])
    nl = -p["lam"][...]
    softplus = jnp.maximum(nl, 0.0) + jnp.log1p(jnp.exp(-jnp.abs(nl)))
    neg_log_a = r * (RG_C * softplus)
    a = jnp.exp2(neg_log_a * NEG_LOG2E)
    mult = jnp.sqrt(jnp.tanh(neg_log_a) * (a * a + 1.0))
    bterm = mult * (i * xc)
    yield
    hseq, h_last = _linear_scan(a, bterm, h_carry[...], a_scr, b_scr, h_scr)
    h_carry[...] = h_last
    yield
    gr = _dot(hn, w_in_ref[:, 2 * G_WIDTH + R_WIDTH:])
    y_r = _rms(_gelu(gr) * hseq, p["g_out_lru"][...]).astype(BF16)
    yield
    y = _dot(jnp.concatenate([y_g, y_r], axis=1), p["w_out"][...])
    h1 = x + _rms(y, p["n_post_mix"][...])

    hn = None
    probs = []
    for hd in range(X_HEADS):
        if hd % 2 == 0:
            yield
            if hn is None:
                hn = _rms(h1, p["n_pre_x"][...]).astype(BF16)
            pair = slice(hd * N_MEM, (hd + 2) * N_MEM)
            sc2 = _dot(hn, p["w_qk"][:, pair])
        sc = sc2[:, (hd % 2) * N_MEM:(hd % 2 + 1) * N_MEM]
        m = jnp.max(sc, axis=-1, keepdims=True)
        e = jnp.exp(sc - m)
        probs.append((e / jnp.sum(e, axis=-1, keepdims=True)).astype(BF16))
    o_cat = jnp.concatenate(probs, axis=1)
    rows = t // OUT_ROW_BLOCKS
    h2s, hn2s = [], []
    for r in range(OUT_ROW_BLOCKS):
        yield
        rs = slice(r * rows, (r + 1) * rows)
        y = _dot(o_cat[rs, :], p["vw"][...])
        h2s.append(h1[rs, :] + _rms(y, p["n_post_x"][...]))
        hn2s.append(_rms(h2s[-1], p["n_pre_ffn"][...]).astype(BF16))
    yield _AFTER_OTHERS
    h_slot[...] = jnp.concatenate(h2s, axis=0)
    hn_buf[...] = jnp.concatenate(hn2s, axis=0)


STAGE_A_COST = (
    (0, 800),
    (1024, 500),
    (1024, 1000),
    (500, 600),
    (1024, 700),
    (512, 1800),
    (0, 800),
    (1024, 1100),
    (2048, 900),
    (1024, 1300),
    (1024, 700),
) + ((2048 // OUT_ROW_BLOCKS, 1700 // OUT_ROW_BLOCKS),) * OUT_ROW_BLOCKS


def _stage_b(h_buf, hn_buf, p, act_scr, o_ref):
    for c in range(D_FF // FF_CHUNK):
        yield
        sl = slice(c * FF_CHUNK, (c + 1) * FF_CHUNK)
        g = _dot(hn_buf[...], p["w_gate"][:, sl])
        up = _dot(hn_buf[...], p["w_up"][:, sl])
        act_scr[:, sl] = (g * _sigmoid(g) * up).astype(BF16)
    rows = act_scr.shape[0] // DOWN_ROW_BLOCKS
    cols = D_MODEL // DOWN_COL_BLOCKS
    for r in range(DOWN_ROW_BLOCKS):
        rs = slice(r * rows, (r + 1) * rows)
        ys = []
        for c in range(DOWN_COL_BLOCKS):
            yield
            ys.append(_dot(act_scr[rs, :], p["w_down"][:, c * cols:(c + 1) * cols]))
        y = jnp.concatenate(ys, axis=1)
        o_ref[rs, :] = h_buf[rs, :] + _rms(y, p["n_post_ffn"][...])


_DOWN_PIECE = (5632 // (DOWN_ROW_BLOCKS * DOWN_COL_BLOCKS), 900 // (DOWN_ROW_BLOCKS * DOWN_COL_BLOCKS))
STAGE_B_COST = (((1024, 250),) * (D_FF // FF_CHUNK)
                + (_DOWN_PIECE,) * (DOWN_ROW_BLOCKS * DOWN_COL_BLOCKS))

_AFTER_OTHERS = object()


def _interleave(lead, lead_cost, fill, fill_cost):
    fill_total = sum(m for m, _ in fill_cost)
    valu_total = sum(v for _, v in lead_cost)
    next(lead)
    next(fill)
    fill_i, fill_done, valu_done, marker = 0, 0.0, 0.0, None
    for k in range(len(lead_cost)):
        marker = next(lead)
        valu_done += lead_cost[k][1]
        target = fill_total * valu_done / valu_total
        while fill_i < len(fill_cost) and fill_done + fill_cost[fill_i][0] / 2 <= target:
            next(fill, None)
            fill_done += fill_cost[fill_i][0]
            fill_i += 1
    if marker is not _AFTER_OTHERS:
        raise ValueError("lead_cost does not match the lead generator's phases")
    for _ in range(fill_i, len(fill_cost)):
        next(fill, None)
    leftovers = [next(fill, None), next(lead, None)]
    if leftovers != [None, None]:
        raise ValueError("phase generators yielded more phases than their cost tables list")


_PARAM_ORDER = (
    "n_pre_mix", "w_in", "ln_v_g", "ln_v_b", "w_s", "b_s", "conv_w", "conv_b", "w_gate_lru",
    "b_a", "b_x", "lam", "g_out_gmlp", "g_out_lru", "w_out", "n_post_mix",
    "n_pre_x", "w_qk", "vw", "n_post_x",
    "n_pre_ffn", "w_gate", "w_up", "w_down", "n_post_ffn",
)
_MATMUL_WEIGHTS = ("w_in", "w_out", "w_gate", "w_up", "w_down")
LOAD_ROWS = 128
LOAD_SLOTS = 4


def _load_weights(hbm_refs, vmem_refs, stage, sems):
    jobs = [(src, dst, r0) for src, dst in zip(hbm_refs, vmem_refs)
            for r0 in range(0, src.shape[0], LOAD_ROWS)]

    def copy(n):
        src, _, r0 = jobs[n]
        slot = n % LOAD_SLOTS
        return pltpu.make_async_copy(src.at[pl.ds(r0, LOAD_ROWS), :],
                                     stage.at[slot, :, pl.ds(0, src.shape[1])], sems.at[slot])

    ahead = LOAD_SLOTS - 1
    for n in range(min(ahead, len(jobs))):
        copy(n).start()
    for n, (src, dst, r0) in enumerate(jobs):
        if n + ahead < len(jobs):
            copy(n + ahead).start()
        copy(n).wait()
        dst[pl.ds(r0, LOAD_ROWS), :] = stage[n % LOAD_SLOTS, :, 0:src.shape[1]].astype(BF16)


def _layer_kernel(tiles_per_seq, x_ref, *rest):
    n_p, n_w = len(_PARAM_ORDER), len(_MATMUL_WEIGHTS)
    p = dict(zip(_PARAM_ORDER, rest[:n_p]))
    o_ref = rest[n_p]
    scratch = rest[n_p + 1:]
    weights, (w_stage, w_sems) = scratch[:n_w], scratch[n_w:n_w + 2]
    h_buf, hn_buf, act_scr, h_carry, xr_scr, a_scr, b_scr, h_scr = scratch[n_w + 2:]
    step = pl.program_id(0)
    n_tiles = pl.num_programs(0) - (PIPELINE_DEPTH - 1)
    tile = jnp.minimum(step, n_tiles - 1)

    @pl.when(step == 0)
    def _():
        _load_weights([p[k] for k in _MATMUL_WEIGHTS], weights, w_stage, w_sems)
        h_buf[...] = jnp.zeros_like(h_buf)
        hn_buf[...] = jnp.zeros_like(hn_buf)

    p.update(zip(_MATMUL_WEIGHTS, weights))

    @pl.when(lax.rem(tile, tiles_per_seq) == 0)
    def _():
        xr_scr[0:SUBLANES, :] = jnp.zeros((SUBLANES, R_WIDTH), F32)
        h_carry[...] = jnp.zeros_like(h_carry)

    stage_a = _stage_a(x_ref, p, h_buf, hn_buf, h_carry, xr_scr, a_scr, b_scr, h_scr)
    stage_b = _stage_b(h_buf, hn_buf, p, act_scr, o_ref)
    _interleave(stage_a, STAGE_A_COST, stage_b, STAGE_B_COST)


def _resident(shape):
    zeros = (0,) * len(shape)
    return pl.BlockSpec(shape, lambda *_: zeros, pipeline_mode=pl.Buffered(1))


def _block_diag(w):
    h, d, _ = w.shape
    eye = jnp.eye(h, dtype=w.dtype)
    return jnp.einsum('hij,hk->hikj', w, eye).reshape(h * d, h * d)


def _row(p):
    return p.reshape(1, -1).astype(F32)


def kernel(x, mem, w_in, ln_v_g, ln_v_b, w_s, b_s, conv_w, conv_b, w_a, b_a, w_x, b_x, lam,
           g_out_gmlp, g_out_lru, w_out, w_q, w_kv, w_o, w_gate, w_up, w_down, n_pre_mix,
           n_post_mix, n_pre_x, n_mem, n_post_x, n_pre_ffn, n_post_ffn):
    batch, seq, _ = x.shape
    depth = w_in.shape[0]
    t = TOKENS
    assert seq % t == 0 and t % CHUNK == 0
    tiles_per_seq = seq // t
    n_tiles = batch * tiles_per_seq
    heads_per_block = GATE_BLOCK // R_HEAD_DIM

    def tile_of_a(i):
        return jnp.minimum(i, n_tiles - 1)

    def tile_of_b(i):
        return jnp.maximum(i - (PIPELINE_DEPTH - 1), 0)

    x_spec = pl.BlockSpec((None, t, D_MODEL),
                          lambda i: (tile_of_a(i) // tiles_per_seq, tile_of_a(i) % tiles_per_seq, 0))
    o_spec = pl.BlockSpec((None, t, D_MODEL),
                          lambda i: (tile_of_b(i) // tiles_per_seq, tile_of_b(i) % tiles_per_seq, 0))
    att_width = X_HEADS * N_MEM
    w_qk_spec = pl.BlockSpec((None, D_MODEL, att_width), lambda i: (tile_of_a(i) // tiles_per_seq, 0, 0),
                             pipeline_mode=pl.Buffered(1))
    vw_spec = pl.BlockSpec((None, att_width, D_MODEL), lambda i: (tile_of_a(i) // tiles_per_seq, 0, 0),
                           pipeline_mode=pl.Buffered(1))

    h = x
    for l in range(depth):
        kv_in = [mem, _row(n_mem[l]), w_kv[l], w_kv[l], w_q[l], w_o[l]]
        w_qk, vw = pl.pallas_call(
            _kv_kernel,
            grid=(X_HEADS, batch),
            in_specs=[pl.BlockSpec((None, N_MEM, D_MODEL), lambda hd, b: (b, 0, 0)),
                      _resident(kv_in[1].shape),
                      pl.BlockSpec((D_MODEL, X_HEAD_DIM), lambda hd, b: (0, hd)),
                      pl.BlockSpec((D_MODEL, X_HEAD_DIM), lambda hd, b: (0, X_HEADS + hd)),
                      pl.BlockSpec((D_MODEL, X_HEAD_DIM), lambda hd, b: (0, hd)),
                      pl.BlockSpec((X_HEAD_DIM, D_MODEL), lambda hd, b: (hd, 0))],
            out_specs=[pl.BlockSpec((None, D_MODEL, N_MEM), lambda hd, b: (b, 0, hd)),
                       pl.BlockSpec((None, N_MEM, D_MODEL), lambda hd, b: (b, hd, 0))],
            out_shape=[jax.ShapeDtypeStruct((batch, D_MODEL, att_width), BF16),
                       jax.ShapeDtypeStruct((batch, att_width, D_MODEL), BF16)],
            compiler_params=pltpu.CompilerParams(dimension_semantics=("arbitrary", "arbitrary"),
                                                 vmem_limit_bytes=VMEM_LIMIT_BYTES),
            name="kv_proj",
        )(*kv_in)

        gate_blocks = []
        for blk in range(R_WIDTH // GATE_BLOCK):
            hs = slice(blk * heads_per_block, (blk + 1) * heads_per_block)
            gate_blocks.append(jnp.concatenate(
                [_block_diag(w_a[l, hs]), _block_diag(w_x[l, hs])], axis=1))

        params = {
            "n_pre_mix": _row(n_pre_mix[l]),
            "w_in": w_in[l],
            "ln_v_g": _row(ln_v_g[l]),
            "ln_v_b": _row(ln_v_b[l]),
            "w_s": w_s[l],
            "b_s": jnp.repeat(b_s[l].T, G_HEAD_DIM, axis=1).astype(F32),
            "conv_w": conv_w[l].astype(F32),
            "conv_b": _row(conv_b[l]),
            "w_gate_lru": jnp.stack(gate_blocks).astype(BF16),
            "b_a": _row(b_a[l]),
            "b_x": _row(b_x[l]),
            "lam": _row(lam[l]),
            "g_out_gmlp": _row(g_out_gmlp[l]),
            "g_out_lru": _row(g_out_lru[l]),
            "w_out": w_out[l],
            "n_post_mix": _row(n_post_mix[l]),
            "n_pre_x": _row(n_pre_x[l]),
            "w_qk": w_qk,
            "vw": vw,
            "n_post_x": _row(n_post_x[l]),
            "n_pre_ffn": _row(n_pre_ffn[l]),
            "w_gate": w_gate[l],
            "w_up": w_up[l],
            "w_down": w_down[l],
            "n_post_ffn": _row(n_post_ffn[l]),
        }
        specs = {"w_qk": w_qk_spec, "vw": vw_spec}
        specs.update({k: pl.BlockSpec(memory_space=pl.ANY) for k in _MATMUL_WEIGHTS})
        in_specs = [x_spec] + [specs.get(k) or _resident(params[k].shape) for k in _PARAM_ORDER]
        stage_cols = max(params[k].shape[1] for k in _MATMUL_WEIGHTS)
        assert all(params[k].shape[0] % LOAD_ROWS == 0 for k in _MATMUL_WEIGHTS)

        def layer_body(*refs):
            _layer_kernel(tiles_per_seq, *refs)

        h = pl.pallas_call(
            layer_body,
            grid=(n_tiles + PIPELINE_DEPTH - 1,),
            in_specs=in_specs,
            out_specs=o_spec,
            out_shape=jax.ShapeDtypeStruct((batch, seq, D_MODEL), F32),
            scratch_shapes=[pltpu.VMEM(params[k].shape, BF16) for k in _MATMUL_WEIGHTS] + [
                pltpu.VMEM((LOAD_SLOTS, LOAD_ROWS, stage_cols), F32),
                pltpu.SemaphoreType.DMA((LOAD_SLOTS,)),
                pltpu.VMEM((t, D_MODEL), F32),
                pltpu.VMEM((t, D_MODEL), BF16),
                pltpu.VMEM((t, D_FF), BF16),
                pltpu.VMEM((1, R_WIDTH), F32),
                pltpu.VMEM((t + SUBLANES, R_WIDTH), F32),
                pltpu.VMEM((t // SUBLANES, SUBLANES, R_WIDTH), F32),
                pltpu.VMEM((t // SUBLANES, SUBLANES, R_WIDTH), F32),
                pltpu.VMEM((t // SUBLANES, SUBLANES, R_WIDTH), F32),
            ],
            compiler_params=pltpu.CompilerParams(dimension_semantics=("arbitrary",),
                                                 vmem_limit_bytes=VMEM_LIMIT_BYTES),
            name="layer",
        )(h, *[params[k] for k in _PARAM_ORDER])
    return h
```

```python
import math

import jax
import jax.numpy as jnp
from jax import lax
from jax.experimental import pallas as pl
from jax.experimental.pallas import tpu as pltpu

D_MODEL = 1024
G_WIDTH = 512
R_WIDTH = 512
G_HEADS = 4
G_HEAD_DIM = 128
CHUNK = 128
R_HEADS = 8
R_HEAD_DIM = 64
CONV_W = 4
RG_C = 8.0
N_MEM = 256
X_HEADS = 4
X_HEAD_DIM = 256
D_FF = 2816
EPS = 1e-6

SUBLANES = 8
NEG_LOG2E = -math.log2(math.e)
GATE_BLOCK = 256
TOKENS = 512
FF_CHUNK = 256
DOWN_ROW_BLOCKS = 2
DOWN_COL_BLOCKS = 2
OUT_ROW_BLOCKS = 2
PIPELINE_DEPTH = 2
VMEM_LIMIT_BYTES = 60 * 1024 * 1024

BF16 = jnp.bfloat16
F32 = jnp.float32


def _rms(x, g):
    return x * lax.rsqrt(jnp.mean(x * x, axis=-1, keepdims=True) + EPS) * g


def _sigmoid(x):
    return 1.0 / (1.0 + jnp.exp2(x * NEG_LOG2E))


def _gelu(x):
    k1 = 2.0 * math.sqrt(2.0 / math.pi) * NEG_LOG2E
    return x / (1.0 + jnp.exp2(x * (k1 + (k1 * 0.044715) * (x * x))))


def _dot(a, b):
    return jnp.dot(a, b, preferred_element_type=F32)


def _kv_kernel(mem_ref, n_mem_ref, w_kv_ref, w_q_ref, w_o_ref, w_qk_ref, vw_ref):
    m = _rms(mem_ref[...], n_mem_ref[...]).astype(BF16)
    width = X_HEADS * X_HEAD_DIM
    k = _dot(m, w_kv_ref[:, :width].astype(BF16))
    v = _dot(m, w_kv_ref[:, width:].astype(BF16))
    for hd in range(X_HEADS):
        dims = slice(hd * X_HEAD_DIM, (hd + 1) * X_HEAD_DIM)
        mems = slice(hd * N_MEM, (hd + 1) * N_MEM)
        kt = k[:, dims].T.astype(BF16)
        w_qk = _dot(w_q_ref[:, dims].astype(BF16), kt) * (X_HEAD_DIM ** -0.5)
        w_qk_ref[:, mems] = w_qk.astype(BF16)
        vw = _dot(v[:, dims].astype(BF16), w_o_ref[dims, :].astype(BF16))
        vw_ref[mems, :] = vw.astype(BF16)


def _linear_scan(a, b, h0, a_scr, b_scr, h_scr):
    t, c = a.shape
    groups = t // SUBLANES
    a3 = a.reshape(groups, SUBLANES, c)
    b3 = b.reshape(groups, SUBLANES, c)
    row = lax.broadcasted_iota(jnp.int32, a3.shape, 1)
    shift = 1
    while shift < SUBLANES:
        keep = row >= shift
        a_prev = jnp.where(keep, pltpu.roll(a3, shift, 1), 1.0)
        b_prev = jnp.where(keep, pltpu.roll(b3, shift, 1), 0.0)
        b3 = a3 * b_prev + b3
        a3 = a3 * a_prev
        shift *= 2
    a_scr[...] = a3
    b_scr[...] = b3
    carry = h0
    for g in range(groups):
        blk = a_scr[g] * carry + b_scr[g]
        h_scr[g] = blk
        carry = blk[SUBLANES - 1:SUBLANES, :]
    return h_scr[...].reshape(t, c), carry


def _stage_a(x_ref, p, h_slot, hn_buf, h_carry, xr_scr, a_scr, b_scr, h_scr):
    yield
    x = x_ref[...]
    t = x.shape[0]
    w_in_ref = p["w_in"]
    hn = _rms(x, p["n_pre_mix"][...]).astype(BF16)

    yield
    u = _gelu(_dot(hn, w_in_ref[:, 0:G_WIDTH]))
    yield
    gv = _gelu(_dot(hn, w_in_ref[:, G_WIDTH:2 * G_WIDTH]))
    mu = jnp.mean(gv, axis=-1, keepdims=True)
    cen = gv - mu
    var = jnp.mean(cen * cen, axis=-1, keepdims=True)
    v = (cen * lax.rsqrt(var + EPS) * p["ln_v_g"][...] + p["ln_v_b"][...]).astype(BF16)
    yield
    tri_r = lax.broadcasted_iota(jnp.int32, (CHUNK, CHUNK), 0)
    tri_c = lax.broadcasted_iota(jnp.int32, (CHUNK, CHUNK), 1)
    causal = tri_r >= tri_c
    ws = [jnp.where(causal, p["w_s"][hd], 0.0).astype(BF16) for hd in range(G_HEADS)]
    bs = p["b_s"][...]
    rows = []
    for c in range(t // CHUNK):
        cols = []
        for hd in range(G_HEADS):
            vb = v[c * CHUNK:(c + 1) * CHUNK, hd * G_HEAD_DIM:(hd + 1) * G_HEAD_DIM]
            cols.append(_dot(ws[hd], vb))
        rows.append(jnp.concatenate(cols, axis=1) + bs)
    s = jnp.concatenate(rows, axis=0)
    y_g = _rms(u * s, p["g_out_gmlp"][...]).astype(BF16)

    yield
    xr = _dot(hn, w_in_ref[:, 2 * G_WIDTH:2 * G_WIDTH + R_WIDTH])
    xr_scr[SUBLANES:SUBLANES + t, :] = xr
    cw = p["conv_w"][...]
    xc = p["conv_b"][...]
    for k in range(CONV_W):
        off = SUBLANES - (CONV_W - 1) + k
        xc = xc + xr_scr[off:off + t, :] * cw[k:k + 1, :]
    xr_scr[0:SUBLANES, :] = xr[t - SUBLANES:t, :]
    xc_b = xc.astype(BF16)
    yield
    ga, gx = [], []
    for blk in range(R_WIDTH // GATE_BLOCK):
        g2 = _dot(xc_b[:, blk * GATE_BLOCK:(blk + 1) * GATE_BLOCK], p["w_gate_lru"][blk])
        ga.append(g2[:, :GATE_BLOCK])
        gx.append(g2[:, GATE_BLOCK:])
    r = _sigmoid(jnp.concatenate(ga, axis=1) + p["b_a"][...])
    i = _sigmoid(jnp.concatenate(gx, axis=1) + p["b_x"][...])
    nl = -p["lam"][...]
    softplus = jnp.maximum(nl, 0.0) + jnp.log1p(jnp.exp(-jnp.abs(nl)))
    neg_log_a = r * (RG_C * softplus)
    a = jnp.exp2(neg_log_a * NEG_LOG2E)
    mult = jnp.sqrt(jnp.tanh(neg_log_a) * (a * a + 1.0))
    bterm = mult * (i * xc)
    yield
    hseq, h_last = _linear_scan(a, bterm, h_carry[...], a_scr, b_scr, h_scr)
    h_carry[...] = h_last
    yield
    gr = _dot(hn, w_in_ref[:, 2 * G_WIDTH + R_WIDTH:])
    y_r = _rms(_gelu(gr) * hseq, p["g_out_lru"][...]).astype(BF16)
    yield
    y = _dot(jnp.concatenate([y_g, y_r], axis=1), p["w_out"][...])
    h1 = x + _rms(y, p["n_post_mix"][...])

    hn = None
    probs = []
    for hd in range(X_HEADS):
        if hd % 2 == 0:
            yield
            if hn is None:
                hn = _rms(h1, p["n_pre_x"][...]).astype(BF16)
            pair = slice(hd * N_MEM, (hd + 2) * N_MEM)
            sc2 = _dot(hn, p["w_qk"][:, pair])
        sc = sc2[:, (hd % 2) * N_MEM:(hd % 2 + 1) * N_MEM]
        m = jnp.max(sc, axis=-1, keepdims=True)
        e = jnp.exp(sc - m)
        probs.append((e / jnp.sum(e, axis=-1, keepdims=True)).astype(BF16))
    o_cat = jnp.concatenate(probs, axis=1)
    rows = t // OUT_ROW_BLOCKS
    h2s, hn2s = [], []
    for r in range(OUT_ROW_BLOCKS):
        yield
        rs = slice(r * rows, (r + 1) * rows)
        y = _dot(o_cat[rs, :], p["vw"][...])
        h2s.append(h1[rs, :] + _rms(y, p["n_post_x"][...]))
        hn2s.append(_rms(h2s[-1], p["n_pre_ffn"][...]).astype(BF16))
    yield _AFTER_OTHERS
    h_slot[...] = jnp.concatenate(h2s, axis=0)
    hn_buf[...] = jnp.concatenate(hn2s, axis=0)


STAGE_A_COST = (
    (0, 800),
    (1024, 500),
    (1024, 1000),
    (500, 600),
    (1024, 700),
    (512, 1800),
    (0, 800),
    (1024, 1100),
    (2048, 900),
    (1024, 1300),
    (1024, 700),
) + ((2048 // OUT_ROW_BLOCKS, 1700 // OUT_ROW_BLOCKS),) * OUT_ROW_BLOCKS


def _stage_b(h_buf, hn_buf, p, act_scr, o_ref):
    for c in range(D_FF // FF_CHUNK):
        yield
        sl = slice(c * FF_CHUNK, (c + 1) * FF_CHUNK)
        g = _dot(hn_buf[...], p["w_gate"][:, sl])
        up = _dot(hn_buf[...], p["w_up"][:, sl])
        act_scr[:, sl] = (g * _sigmoid(g) * up).astype(BF16)
    rows = act_scr.shape[0] // DOWN_ROW_BLOCKS
    cols = D_MODEL // DOWN_COL_BLOCKS
    for r in range(DOWN_ROW_BLOCKS):
        rs = slice(r * rows, (r + 1) * rows)
        ys = []
        for c in range(DOWN_COL_BLOCKS):
            yield
            ys.append(_dot(act_scr[rs, :], p["w_down"][:, c * cols:(c + 1) * cols]))
        y = jnp.concatenate(ys, axis=1)
        o_ref[rs, :] = h_buf[rs, :] + _rms(y, p["n_post_ffn"][...])


_DOWN_PIECE = (5632 // (DOWN_ROW_BLOCKS * DOWN_COL_BLOCKS), 900 // (DOWN_ROW_BLOCKS * DOWN_COL_BLOCKS))
STAGE_B_COST = (((1024, 250),) * (D_FF // FF_CHUNK)
                + (_DOWN_PIECE,) * (DOWN_ROW_BLOCKS * DOWN_COL_BLOCKS))

_AFTER_OTHERS = object()


def _interleave(lead, lead_cost, fill, fill_cost):
    fill_total = sum(m for m, _ in fill_cost)
    valu_total = sum(v for _, v in lead_cost)
    next(lead)
    next(fill)
    fill_i, fill_done, valu_done, marker = 0, 0.0, 0.0, None
    for k in range(len(lead_cost)):
        marker = next(lead)
        valu_done += lead_cost[k][1]
        target = fill_total * valu_done / valu_total
        while fill_i < len(fill_cost) and fill_done + fill_cost[fill_i][0] / 2 <= target:
            next(fill, None)
            fill_done += fill_cost[fill_i][0]
            fill_i += 1
    if marker is not _AFTER_OTHERS:
        raise ValueError("lead_cost does not match the lead generator's phases")
    for _ in range(fill_i, len(fill_cost)):
        next(fill, None)
    leftovers = [next(fill, None), next(lead, None)]
    if leftovers != [None, None]:
        raise ValueError("phase generators yielded more phases than their cost tables list")


_PARAM_ORDER = (
    "n_pre_mix", "w_in", "ln_v_g", "ln_v_b", "w_s", "b_s", "conv_w", "conv_b", "w_gate_lru",
    "b_a", "b_x", "lam", "g_out_gmlp", "g_out_lru", "w_out", "n_post_mix",
    "n_pre_x", "w_qk", "vw", "n_post_x",
    "n_pre_ffn", "w_gate", "w_up", "w_down", "n_post_ffn",
)
_MATMUL_WEIGHTS = ("w_in", "w_out", "w_gate", "w_up", "w_down")
LOAD_ROWS = 128
LOAD_SLOTS = 4


def _load_weights(hbm_refs, vmem_refs, stage, sems):
    jobs = [(src, dst, r0) for src, dst in zip(hbm_refs, vmem_refs)
            for r0 in range(0, src.shape[0], LOAD_ROWS)]

    def copy(n):
        src, _, r0 = jobs[n]
        slot = n % LOAD_SLOTS
        return pltpu.make_async_copy(src.at[pl.ds(r0, LOAD_ROWS), :],
                                     stage.at[slot, :, pl.ds(0, src.shape[1])], sems.at[slot])

    ahead = LOAD_SLOTS - 1
    for n in range(min(ahead, len(jobs))):
        copy(n).start()
    for n, (src, dst, r0) in enumerate(jobs):
        if n + ahead < len(jobs):
            copy(n + ahead).start()
        copy(n).wait()
        dst[pl.ds(r0, LOAD_ROWS), :] = stage[n % LOAD_SLOTS, :, 0:src.shape[1]].astype(BF16)


def _layer_kernel(tiles_per_seq, x_ref, *rest):
    n_p, n_w = len(_PARAM_ORDER), len(_MATMUL_WEIGHTS)
    p = dict(zip(_PARAM_ORDER, rest[:n_p]))
    o_ref = rest[n_p]
    scratch = rest[n_p + 1:]
    weights, (w_stage, w_sems) = scratch[:n_w], scratch[n_w:n_w + 2]
    h_buf, hn_buf, act_scr, h_carry, xr_scr, a_scr, b_scr, h_scr = scratch[n_w + 2:]
    step = pl.program_id(0)
    n_tiles = pl.num_programs(0) - (PIPELINE_DEPTH - 1)
    tile = jnp.minimum(step, n_tiles - 1)

    @pl.when(step == 0)
    def _():
        _load_weights([p[k] for k in _MATMUL_WEIGHTS], weights, w_stage, w_sems)
        h_buf[...] = jnp.zeros_like(h_buf)
        hn_buf[...] = jnp.zeros_like(hn_buf)

    p.update(zip(_MATMUL_WEIGHTS, weights))

    @pl.when(lax.rem(tile, tiles_per_seq) == 0)
    def _():
        xr_scr[0:SUBLANES, :] = jnp.zeros((SUBLANES, R_WIDTH), F32)
        h_carry[...] = jnp.zeros_like(h_carry)

    stage_a = _stage_a(x_ref, p, h_buf, hn_buf, h_carry, xr_scr, a_scr, b_scr, h_scr)
    stage_b = _stage_b(h_buf, hn_buf, p, act_scr, o_ref)
    _interleave(stage_a, STAGE_A_COST, stage_b, STAGE_B_COST)


def _resident(shape):
    zeros = (0,) * len(shape)
    return pl.BlockSpec(shape, lambda *_: zeros, pipeline_mode=pl.Buffered(1))


def _block_diag(w):
    h, d, _ = w.shape
    eye = jnp.eye(h, dtype=w.dtype)
    return jnp.einsum('hij,hk->hikj', w, eye).reshape(h * d, h * d)


def _row(p):
    return p.reshape(1, -1).astype(F32)


def kernel(x, mem, w_in, ln_v_g, ln_v_b, w_s, b_s, conv_w, conv_b, w_a, b_a, w_x, b_x, lam,
           g_out_gmlp, g_out_lru, w_out, w_q, w_kv, w_o, w_gate, w_up, w_down, n_pre_mix,
           n_post_mix, n_pre_x, n_mem, n_post_x, n_pre_ffn, n_post_ffn):
    batch, seq, _ = x.shape
    depth = w_in.shape[0]
    t = TOKENS
    assert seq % t == 0 and t % CHUNK == 0
    tiles_per_seq = seq // t
    n_tiles = batch * tiles_per_seq
    heads_per_block = GATE_BLOCK // R_HEAD_DIM

    def tile_of_a(i):
        return jnp.minimum(i, n_tiles - 1)

    def tile_of_b(i):
        return jnp.maximum(i - (PIPELINE_DEPTH - 1), 0)

    x_spec = pl.BlockSpec((None, t, D_MODEL),
                          lambda i: (tile_of_a(i) // tiles_per_seq, tile_of_a(i) % tiles_per_seq, 0))
    o_spec = pl.BlockSpec((None, t, D_MODEL),
                          lambda i: (tile_of_b(i) // tiles_per_seq, tile_of_b(i) % tiles_per_seq, 0))
    att_width = X_HEADS * N_MEM
    w_qk_spec = pl.BlockSpec((None, D_MODEL, att_width), lambda i: (tile_of_a(i) // tiles_per_seq, 0, 0),
                             pipeline_mode=pl.Buffered(1))
    vw_spec = pl.BlockSpec((None, att_width, D_MODEL), lambda i: (tile_of_a(i) // tiles_per_seq, 0, 0),
                           pipeline_mode=pl.Buffered(1))

    h = x
    for l in range(depth):
        kv_in = [mem, _row(n_mem[l]), w_kv[l], w_q[l], w_o[l]]
        w_qk, vw = pl.pallas_call(
            _kv_kernel,
            grid=(batch,),
            in_specs=[pl.BlockSpec((None, N_MEM, D_MODEL), lambda b: (b, 0, 0))]
                     + [_resident(a.shape) for a in kv_in[1:]],
            out_specs=[pl.BlockSpec((None, D_MODEL, att_width), lambda b: (b, 0, 0)),
                       pl.BlockSpec((None, att_width, D_MODEL), lambda b: (b, 0, 0))],
            out_shape=[jax.ShapeDtypeStruct((batch, D_MODEL, att_width), BF16),
                       jax.ShapeDtypeStruct((batch, att_width, D_MODEL), BF16)],
            compiler_params=pltpu.CompilerParams(dimension_semantics=("arbitrary",),
                                                 vmem_limit_bytes=VMEM_LIMIT_BYTES),
            name="kv_proj",
        )(*kv_in)

        gate_blocks = []
        for blk in range(R_WIDTH // GATE_BLOCK):
            hs = slice(blk * heads_per_block, (blk + 1) * heads_per_block)
            gate_blocks.append(jnp.concatenate(
                [_block_diag(w_a[l, hs]), _block_diag(w_x[l, hs])], axis=1))

        params = {
            "n_pre_mix": _row(n_pre_mix[l]),
            "w_in": w_in[l],
            "ln_v_g": _row(ln_v_g[l]),
            "ln_v_b": _row(ln_v_b[l]),
            "w_s": w_s[l],
            "b_s": jnp.repeat(b_s[l].T, G_HEAD_DIM, axis=1).astype(F32),
            "conv_w": conv_w[l].astype(F32),
            "conv_b": _row(conv_b[l]),
            "w_gate_lru": jnp.stack(gate_blocks).astype(BF16),
            "b_a": _row(b_a[l]),
            "b_x": _row(b_x[l]),
            "lam": _row(lam[l]),
            "g_out_gmlp": _row(g_out_gmlp[l]),
            "g_out_lru": _row(g_out_lru[l]),
            "w_out": w_out[l],
            "n_post_mix": _row(n_post_mix[l]),
            "n_pre_x": _row(n_pre_x[l]),
            "w_qk": w_qk,
            "vw": vw,
            "n_post_x": _row(n_post_x[l]),
            "n_pre_ffn": _row(n_pre_ffn[l]),
            "w_gate": w_gate[l],
            "w_up": w_up[l],
            "w_down": w_down[l],
            "n_post_ffn": _row(n_post_ffn[l]),
        }
        specs = {"w_qk": w_qk_spec, "vw": vw_spec}
        specs.update({k: pl.BlockSpec(memory_space=pl.ANY) for k in _MATMUL_WEIGHTS})
        in_specs = [x_spec] + [specs.get(k) or _resident(params[k].shape) for k in _PARAM_ORDER]
        stage_cols = max(params[k].shape[1] for k in _MATMUL_WEIGHTS)
        assert all(params[k].shape[0] % LOAD_ROWS == 0 for k in _MATMUL_WEIGHTS)

        def layer_body(*refs):
            _layer_kernel(tiles_per_seq, *refs)

        h = pl.pallas_call(
            layer_body,
            grid=(n_tiles + PIPELINE_DEPTH - 1,),
            in_specs=in_specs,
            out_specs=o_spec,
            out_shape=jax.ShapeDtypeStruct((batch, seq, D_MODEL), F32),
            scratch_shapes=[pltpu.VMEM(params[k].shape, BF16) for k in _MATMUL_WEIGHTS] + [
                pltpu.VMEM((LOAD_SLOTS, LOAD_ROWS, stage_cols), F32),
                pltpu.SemaphoreType.DMA((LOAD_SLOTS,)),
                pltpu.VMEM((t, D_MODEL), F32),
                pltpu.VMEM((t, D_MODEL), BF16),
                pltpu.VMEM((t, D_FF), BF16),
                pltpu.VMEM((1, R_WIDTH), F32),
                pltpu.VMEM((t + SUBLANES, R_WIDTH), F32),
                pltpu.VMEM((t // SUBLANES, SUBLANES, R_WIDTH), F32),
                pltpu.VMEM((t // SUBLANES, SUBLANES, R_WIDTH), F32),
                pltpu.VMEM((t // SUBLANES, SUBLANES, R_WIDTH), F32),
            ],
            compiler_params=pltpu.CompilerParams(dimension_semantics=("arbitrary",),
                                                 vmem_limit_bytes=VMEM_LIMIT_BYTES),
            name="layer",
        )(h, *[params[k] for k in _PARAM_ORDER])
    return h
```

```python
import math

import jax
import jax.numpy as jnp
from jax import lax
from jax.experimental import pallas as pl
from jax.experimental.pallas import tpu as pltpu

D_MODEL = 1024
G_WIDTH = 512
R_WIDTH = 512
G_HEADS = 4
G_HEAD_DIM = 128
CHUNK = 128
R_HEADS = 8
R_HEAD_DIM = 64
CONV_W = 4
RG_C = 8.0
N_MEM = 256
X_HEADS = 4
X_HEAD_DIM = 256
D_FF = 2816
EPS = 1e-6

SUBLANES = 8
NEG_LOG2E = -math.log2(math.e)
GATE_BLOCK = 256
TOKENS = 512
FF_CHUNK = 256
DOWN_ROW_BLOCKS = 2
DOWN_COL_BLOCKS = 2
OUT_ROW_BLOCKS = 2
PIPELINE_DEPTH = 2
VMEM_LIMIT_BYTES = 60 * 1024 * 1024

BF16 = jnp.bfloat16
F32 = jnp.float32


def _rms(x, g):
    return x * lax.rsqrt(jnp.mean(x * x, axis=-1, keepdims=True) + EPS) * g


def _sigmoid(x):
    return 1.0 / (1.0 + jnp.exp2(x * NEG_LOG2E))


def _gelu(x):
    k1 = 2.0 * math.sqrt(2.0 / math.pi) * NEG_LOG2E
    return x / (1.0 + jnp.exp2(x * (k1 + (k1 * 0.044715) * (x * x))))


def _dot(a, b):
    return jnp.dot(a, b, preferred_element_type=F32)


def _kv_kernel(mem_ref, n_mem_ref, w_kv_ref, w_q_ref, w_o_ref, w_qk_ref, vw_ref):
    m = _rms(mem_ref[...], n_mem_ref[...]).astype(BF16)
    width = X_HEADS * X_HEAD_DIM
    k = _dot(m, w_kv_ref[:, :width].astype(BF16))
    v = _dot(m, w_kv_ref[:, width:].astype(BF16))
    for hd in range(X_HEADS):
        dims = slice(hd * X_HEAD_DIM, (hd + 1) * X_HEAD_DIM)
        mems = slice(hd * N_MEM, (hd + 1) * N_MEM)
        kt = k[:, dims].T.astype(BF16)
        w_qk = _dot(w_q_ref[:, dims].astype(BF16), kt) * (X_HEAD_DIM ** -0.5)
        w_qk_ref[:, mems] = w_qk.astype(BF16)
        vw = _dot(v[:, dims].astype(BF16), w_o_ref[dims, :].astype(BF16))
        vw_ref[mems, :] = vw.astype(BF16)


def _linear_scan(a, b, h0, a_scr, b_scr, h_scr):
    t, c = a.shape
    groups = t // SUBLANES
    a3 = a.reshape(groups, SUBLANES, c)
    b3 = b.reshape(groups, SUBLANES, c)
    row = lax.broadcasted_iota(jnp.int32, a3.shape, 1)
    shift = 1
    while shift < SUBLANES:
        keep = row >= shift
        a_prev = jnp.where(keep, pltpu.roll(a3, shift, 1), 1.0)
        b_prev = jnp.where(keep, pltpu.roll(b3, shift, 1), 0.0)
        b3 = a3 * b_prev + b3
        a3 = a3 * a_prev
        shift *= 2
    a_scr[...] = a3
    b_scr[...] = b3
    carry = h0
    for g in range(groups):
        blk = a_scr[g] * carry + b_scr[g]
        h_scr[g] = blk
        carry = blk[SUBLANES - 1:SUBLANES, :]
    return h_scr[...].reshape(t, c), carry


def _stage_a(x_ref, p, h_slot, hn_buf, h_carry, xr_scr, a_scr, b_scr, h_scr):
    yield
    x = x_ref[...]
    t = x.shape[0]
    w_in_ref = p["w_in"]
    hn = _rms(x, p["n_pre_mix"][...]).astype(BF16)

    yield
    u = _gelu(_dot(hn, w_in_ref[:, 0:G_WIDTH]))
    yield
    gv = _gelu(_dot(hn, w_in_ref[:, G_WIDTH:2 * G_WIDTH]))
    mu = jnp.mean(gv, axis=-1, keepdims=True)
    cen = gv - mu
    var = jnp.mean(cen * cen, axis=-1, keepdims=True)
    v = (cen * lax.rsqrt(var + EPS) * p["ln_v_g"][...] + p["ln_v_b"][...]).astype(BF16)
    yield
    tri_r = lax.broadcasted_iota(jnp.int32, (CHUNK, CHUNK), 0)
    tri_c = lax.broadcasted_iota(jnp.int32, (CHUNK, CHUNK), 1)
    causal = tri_r >= tri_c
    ws = [jnp.where(causal, p["w_s"][hd], 0.0).astype(BF16) for hd in range(G_HEADS)]
    bs = p["b_s"][...]
    rows = []
    for c in range(t // CHUNK):
        cols = []
        for hd in range(G_HEADS):
            vb = v[c * CHUNK:(c + 1) * CHUNK, hd * G_HEAD_DIM:(hd + 1) * G_HEAD_DIM]
            cols.append(_dot(ws[hd], vb))
        rows.append(jnp.concatenate(cols, axis=1) + bs)
    s = jnp.concatenate(rows, axis=0)
    y_g = _rms(u * s, p["g_out_gmlp"][...]).astype(BF16)

    yield
    xr = _dot(hn, w_in_ref[:, 2 * G_WIDTH:2 * G_WIDTH + R_WIDTH])
    xr_scr[SUBLANES:SUBLANES + t, :] = xr
    cw = p["conv_w"][...]
    xc = p["conv_b"][...]
    for k in range(CONV_W):
        off = SUBLANES - (CONV_W - 1) + k
        xc = xc + xr_scr[off:off + t, :] * cw[k:k + 1, :]
    xr_scr[0:SUBLANES, :] = xr[t - SUBLANES:t, :]
    xc_b = xc.astype(BF16)
    yield
    ga, gx = [], []
    for blk in range(R_WIDTH // GATE_BLOCK):
        g2 = _dot(xc_b[:, blk * GATE_BLOCK:(blk + 1) * GATE_BLOCK], p["w_gate_lru"][blk])
        ga.append(g2[:, :GATE_BLOCK])
        gx.append(g2[:, GATE_BLOCK:])
    r = _sigmoid(jnp.concatenate(ga, axis=1) + p["b_a"][...])
    i = _sigmoid(jnp.concatenate(gx, axis=1) + p["b_x"][...])
    nl = -p["lam"][...]
    softplus = jnp.maximum(nl, 0.0) + jnp.log1p(jnp.exp(-jnp.abs(nl)))
    neg_log_a = r * (RG_C * softplus)
    a = jnp.exp2(neg_log_a * NEG_LOG2E)
    mult = jnp.sqrt(jnp.tanh(neg_log_a) * (a * a + 1.0))
    bterm = mult * (i * xc)
    yield
    hseq, h_last = _linear_scan(a, bterm, h_carry[...], a_scr, b_scr, h_scr)
    h_carry[...] = h_last
    yield
    gr = _dot(hn, w_in_ref[:, 2 * G_WIDTH + R_WIDTH:])
    y_r = _rms(_gelu(gr) * hseq, p["g_out_lru"][...]).astype(BF16)
    y_cat = jnp.concatenate([y_g, y_r], axis=1)
    rows = t // OUT_ROW_BLOCKS
    blocks = [slice(r * rows, (r + 1) * rows) for r in range(OUT_ROW_BLOCKS)]
    h1s = []
    for rs in blocks:
        yield
        y = _dot(y_cat[rs, :], p["w_out"][...])
        h1s.append(x[rs, :] + _rms(y, p["n_post_mix"][...]))

    probs = []
    for h1 in h1s:
        yield
        hn = _rms(h1, p["n_pre_x"][...]).astype(BF16)
        sc_all = _dot(hn, p["w_qk"][...])
        pr = []
        for hd in range(X_HEADS):
            sc = sc_all[:, hd * N_MEM:(hd + 1) * N_MEM]
            m = jnp.max(sc, axis=-1, keepdims=True)
            e = jnp.exp(sc - m)
            pr.append((e / jnp.sum(e, axis=-1, keepdims=True)).astype(BF16))
        probs.append(jnp.concatenate(pr, axis=1))
    h2s, hn2s = [], []
    for h1, pr in zip(h1s, probs):
        yield
        y = _dot(pr, p["vw"][...])
        h2s.append(h1 + _rms(y, p["n_post_x"][...]))
        hn2s.append(_rms(h2s[-1], p["n_pre_ffn"][...]).astype(BF16))
    yield _AFTER_OTHERS
    h_slot[...] = jnp.concatenate(h2s, axis=0)
    hn_buf[...] = jnp.concatenate(hn2s, axis=0)


STAGE_A_COST = (
    (0, 800),
    (1024, 500),
    (1024, 1000),
    (500, 600),
    (1024, 700),
    (512, 1800),
    (0, 800),
    (1024, 1100),
) + (
    ((2048 // OUT_ROW_BLOCKS, 900 // OUT_ROW_BLOCKS),) * OUT_ROW_BLOCKS
    + ((2048 // OUT_ROW_BLOCKS, 2000 // OUT_ROW_BLOCKS),) * OUT_ROW_BLOCKS
    + ((2048 // OUT_ROW_BLOCKS, 1700 // OUT_ROW_BLOCKS),) * OUT_ROW_BLOCKS
)


def _stage_b(h_buf, hn_buf, p, act_scr, o_ref):
    for c in range(D_FF // FF_CHUNK):
        yield
        sl = slice(c * FF_CHUNK, (c + 1) * FF_CHUNK)
        g = _dot(hn_buf[...], p["w_gate"][:, sl])
        up = _dot(hn_buf[...], p["w_up"][:, sl])
        act_scr[:, sl] = (g * _sigmoid(g) * up).astype(BF16)
    rows = act_scr.shape[0] // DOWN_ROW_BLOCKS
    cols = D_MODEL // DOWN_COL_BLOCKS
    for r in range(DOWN_ROW_BLOCKS):
        rs = slice(r * rows, (r + 1) * rows)
        ys = []
        for c in range(DOWN_COL_BLOCKS):
            yield
            ys.append(_dot(act_scr[rs, :], p["w_down"][:, c * cols:(c + 1) * cols]))
        y = jnp.concatenate(ys, axis=1)
        o_ref[rs, :] = h_buf[rs, :] + _rms(y, p["n_post_ffn"][...])


_DOWN_PIECE = (5632 // (DOWN_ROW_BLOCKS * DOWN_COL_BLOCKS), 900 // (DOWN_ROW_BLOCKS * DOWN_COL_BLOCKS))
STAGE_B_COST = (((1024, 250),) * (D_FF // FF_CHUNK)
                + (_DOWN_PIECE,) * (DOWN_ROW_BLOCKS * DOWN_COL_BLOCKS))

_AFTER_OTHERS = object()


def _interleave(lead, lead_cost, fill, fill_cost):
    fill_total = sum(m for m, _ in fill_cost)
    valu_total = sum(v for _, v in lead_cost)
    next(lead)
    next(fill)
    fill_i, fill_done, valu_done, marker = 0, 0.0, 0.0, None
    for k in range(len(lead_cost)):
        marker = next(lead)
        valu_done += lead_cost[k][1]
        target = fill_total * valu_done / valu_total
        while fill_i < len(fill_cost) and fill_done + fill_cost[fill_i][0] / 2 <= target:
            next(fill, None)
            fill_done += fill_cost[fill_i][0]
            fill_i += 1
    if marker is not _AFTER_OTHERS:
        raise ValueError("lead_cost does not match the lead generator's phases")
    for _ in range(fill_i, len(fill_cost)):
        next(fill, None)
    leftovers = [next(fill, None), next(lead, None)]
    if leftovers != [None, None]:
        raise ValueError("phase generators yielded more phases than their cost tables list")


_PARAM_ORDER = (
    "n_pre_mix", "w_in", "ln_v_g", "ln_v_b", "w_s", "b_s", "conv_w", "conv_b", "w_gate_lru",
    "b_a", "b_x", "lam", "g_out_gmlp", "g_out_lru", "w_out", "n_post_mix",
    "n_pre_x", "w_qk", "vw", "n_post_x",
    "n_pre_ffn", "w_gate", "w_up", "w_down", "n_post_ffn",
)
_MATMUL_WEIGHTS = ("w_in", "w_out", "w_gate", "w_up", "w_down")
LOAD_ROWS = 128
LOAD_SLOTS = 4


def _load_weights(hbm_refs, vmem_refs, stage, sems):
    jobs = [(src, dst, r0) for src, dst in zip(hbm_refs, vmem_refs)
            for r0 in range(0, src.shape[0], LOAD_ROWS)]

    def copy(n):
        src, _, r0 = jobs[n]
        slot = n % LOAD_SLOTS
        return pltpu.make_async_copy(src.at[pl.ds(r0, LOAD_ROWS), :],
                                     stage.at[slot, :, pl.ds(0, src.shape[1])], sems.at[slot])

    ahead = LOAD_SLOTS - 1
    for n in range(min(ahead, len(jobs))):
        copy(n).start()
    for n, (src, dst, r0) in enumerate(jobs):
        if n + ahead < len(jobs):
            copy(n + ahead).start()
        copy(n).wait()
        dst[pl.ds(r0, LOAD_ROWS), :] = stage[n % LOAD_SLOTS, :, 0:src.shape[1]].astype(BF16)


def _layer_kernel(tiles_per_seq, x_ref, *rest):
    n_p, n_w = len(_PARAM_ORDER), len(_MATMUL_WEIGHTS)
    p = dict(zip(_PARAM_ORDER, rest[:n_p]))
    o_ref = rest[n_p]
    scratch = rest[n_p + 1:]
    weights, (w_stage, w_sems) = scratch[:n_w], scratch[n_w:n_w + 2]
    h_buf, hn_buf, act_scr, h_carry, xr_scr, a_scr, b_scr, h_scr = scratch[n_w + 2:]
    step = pl.program_id(0)
    n_tiles = pl.num_programs(0) - (PIPELINE_DEPTH - 1)
    tile = jnp.minimum(step, n_tiles - 1)

    @pl.when(step == 0)
    def _():
        _load_weights([p[k] for k in _MATMUL_WEIGHTS], weights, w_stage, w_sems)
        h_buf[...] = jnp.zeros_like(h_buf)
        hn_buf[...] = jnp.zeros_like(hn_buf)

    p.update(zip(_MATMUL_WEIGHTS, weights))

    @pl.when(lax.rem(tile, tiles_per_seq) == 0)
    def _():
        xr_scr[0:SUBLANES, :] = jnp.zeros((SUBLANES, R_WIDTH), F32)
        h_carry[...] = jnp.zeros_like(h_carry)

    stage_a = _stage_a(x_ref, p, h_buf, hn_buf, h_carry, xr_scr, a_scr, b_scr, h_scr)
    stage_b = _stage_b(h_buf, hn_buf, p, act_scr, o_ref)
    _interleave(stage_a, STAGE_A_COST, stage_b, STAGE_B_COST)


def _resident(shape):
    zeros = (0,) * len(shape)
    return pl.BlockSpec(shape, lambda *_: zeros, pipeline_mode=pl.Buffered(1))


def _block_diag(w):
    h, d, _ = w.shape
    eye = jnp.eye(h, dtype=w.dtype)
    return jnp.einsum('hij,hk->hikj', w, eye).reshape(h * d, h * d)


def _row(p):
    return p.reshape(1, -1).astype(F32)


def kernel(x, mem, w_in, ln_v_g, ln_v_b, w_s, b_s, conv_w, conv_b, w_a, b_a, w_x, b_x, lam,
           g_out_gmlp, g_out_lru, w_out, w_q, w_kv, w_o, w_gate, w_up, w_down, n_pre_mix,
           n_post_mix, n_pre_x, n_mem, n_post_x, n_pre_ffn, n_post_ffn):
    batch, seq, _ = x.shape
    depth = w_in.shape[0]
    t = TOKENS
    assert seq % t == 0 and t % CHUNK == 0
    tiles_per_seq = seq // t
    n_tiles = batch * tiles_per_seq
    heads_per_block = GATE_BLOCK // R_HEAD_DIM

    def tile_of_a(i):
        return jnp.minimum(i, n_tiles - 1)

    def tile_of_b(i):
        return jnp.maximum(i - (PIPELINE_DEPTH - 1), 0)

    x_spec = pl.BlockSpec((None, t, D_MODEL),
                          lambda i: (tile_of_a(i) // tiles_per_seq, tile_of_a(i) % tiles_per_seq, 0))
    o_spec = pl.BlockSpec((None, t, D_MODEL),
                          lambda i: (tile_of_b(i) // tiles_per_seq, tile_of_b(i) % tiles_per_seq, 0))
    att_width = X_HEADS * N_MEM
    w_qk_spec = pl.BlockSpec((None, D_MODEL, att_width), lambda i: (tile_of_a(i) // tiles_per_seq, 0, 0),
                             pipeline_mode=pl.Buffered(1))
    vw_spec = pl.BlockSpec((None, att_width, D_MODEL), lambda i: (tile_of_a(i) // tiles_per_seq, 0, 0),
                           pipeline_mode=pl.Buffered(1))

    h = x
    for l in range(depth):
        kv_in = [mem, _row(n_mem[l]), w_kv[l], w_q[l], w_o[l]]
        w_qk, vw = pl.pallas_call(
            _kv_kernel,
            grid=(batch,),
            in_specs=[pl.BlockSpec((None, N_MEM, D_MODEL), lambda b: (b, 0, 0))]
                     + [_resident(a.shape) for a in kv_in[1:]],
            out_specs=[pl.BlockSpec((None, D_MODEL, att_width), lambda b: (b, 0, 0)),
                       pl.BlockSpec((None, att_width, D_MODEL), lambda b: (b, 0, 0))],
            out_shape=[jax.ShapeDtypeStruct((batch, D_MODEL, att_width), BF16),
                       jax.ShapeDtypeStruct((batch, att_width, D_MODEL), BF16)],
            compiler_params=pltpu.CompilerParams(dimension_semantics=("arbitrary",),
                                                 vmem_limit_bytes=VMEM_LIMIT_BYTES),
            name="kv_proj",
        )(*kv_in)

        gate_blocks = []
        for blk in range(R_WIDTH // GATE_BLOCK):
            hs = slice(blk * heads_per_block, (blk + 1) * heads_per_block)
            gate_blocks.append(jnp.concatenate(
                [_block_diag(w_a[l, hs]), _block_diag(w_x[l, hs])], axis=1))

        params = {
            "n_pre_mix": _row(n_pre_mix[l]),
            "w_in": w_in[l],
            "ln_v_g": _row(ln_v_g[l]),
            "ln_v_b": _row(ln_v_b[l]),
            "w_s": w_s[l],
            "b_s": jnp.repeat(b_s[l].T, G_HEAD_DIM, axis=1).astype(F32),
            "conv_w": conv_w[l].astype(F32),
            "conv_b": _row(conv_b[l]),
            "w_gate_lru": jnp.stack(gate_blocks).astype(BF16),
            "b_a": _row(b_a[l]),
            "b_x": _row(b_x[l]),
            "lam": _row(lam[l]),
            "g_out_gmlp": _row(g_out_gmlp[l]),
            "g_out_lru": _row(g_out_lru[l]),
            "w_out": w_out[l],
            "n_post_mix": _row(n_post_mix[l]),
            "n_pre_x": _row(n_pre_x[l]),
            "w_qk": w_qk,
            "vw": vw,
            "n_post_x": _row(n_post_x[l]),
            "n_pre_ffn": _row(n_pre_ffn[l]),
            "w_gate": w_gate[l],
            "w_up": w_up[l],
            "w_down": w_down[l],
            "n_post_ffn": _row(n_post_ffn[l]),
        }
        specs = {"w_qk": w_qk_spec, "vw": vw_spec}
        specs.update({k: pl.BlockSpec(memory_space=pl.ANY) for k in _MATMUL_WEIGHTS})
        in_specs = [x_spec] + [specs.get(k) or _resident(params[k].shape) for k in _PARAM_ORDER]
        stage_cols = max(params[k].shape[1] for k in _MATMUL_WEIGHTS)
        assert all(params[k].shape[0] % LOAD_ROWS == 0 for k in _MATMUL_WEIGHTS)

        def layer_body(*refs):
            _layer_kernel(tiles_per_seq, *refs)

        h = pl.pallas_call(
            layer_body,
            grid=(n_tiles + PIPELINE_DEPTH - 1,),
            in_specs=in_specs,
            out_specs=o_spec,
            out_shape=jax.ShapeDtypeStruct((batch, seq, D_MODEL), F32),
            scratch_shapes=[pltpu.VMEM(params[k].shape, BF16) for k in _MATMUL_WEIGHTS] + [
                pltpu.VMEM((LOAD_SLOTS, LOAD_ROWS, stage_cols), F32),
                pltpu.SemaphoreType.DMA((LOAD_SLOTS,)),
                pltpu.VMEM((t, D_MODEL), F32),
                pltpu.VMEM((t, D_MODEL), BF16),
                pltpu.VMEM((t, D_FF), BF16),
                pltpu.VMEM((1, R_WIDTH), F32),
                pltpu.VMEM((t + SUBLANES, R_WIDTH), F32),
                pltpu.VMEM((t // SUBLANES, SUBLANES, R_WIDTH), F32),
                pltpu.VMEM((t // SUBLANES, SUBLANES, R_WIDTH), F32),
                pltpu.VMEM((t // SUBLANES, SUBLANES, R_WIDTH), F32),
            ],
            compiler_params=pltpu.CompilerParams(dimension_semantics=("arbitrary",),
                                                 vmem_limit_bytes=VMEM_LIMIT_BYTES),
            name="layer",
        )(h, *[params[k] for k in _PARAM_ORDER])
    return h
```

```python
import math

import jax
import jax.numpy as jnp
from jax import lax
from jax.experimental import pallas as pl
from jax.experimental.pallas import tpu as pltpu

D_MODEL = 1024
G_WIDTH = 512
R_WIDTH = 512
G_HEADS = 4
G_HEAD_DIM = 128
CHUNK = 128
R_HEADS = 8
R_HEAD_DIM = 64
CONV_W = 4
RG_C = 8.0
N_MEM = 256
X_HEADS = 4
X_HEAD_DIM = 256
D_FF = 2816
EPS = 1e-6

SUBLANES = 8
NEG_LOG2E = -math.log2(math.e)
GATE_BLOCK = 256
TOKENS = 512
FF_CHUNK = 256
DOWN_ROW_BLOCKS = 2
DOWN_COL_BLOCKS = 2
ROW_HALVES = 2
OUT_ROW_BLOCKS = 1
PIPELINE_DEPTH = 2
VMEM_LIMIT_BYTES = 60 * 1024 * 1024

BF16 = jnp.bfloat16
F32 = jnp.float32


def _rms(x, g):
    return x * lax.rsqrt(jnp.mean(x * x, axis=-1, keepdims=True) + EPS) * g


def _sigmoid(x):
    return 1.0 / (1.0 + jnp.exp2(x * NEG_LOG2E))


def _gelu(x):
    k1 = 2.0 * math.sqrt(2.0 / math.pi) * NEG_LOG2E
    return x / (1.0 + jnp.exp2(x * (k1 + (k1 * 0.044715) * (x * x))))


def _dot(a, b):
    return jnp.dot(a, b, preferred_element_type=F32)


def _kv_kernel(mem_ref, n_mem_ref, w_kv_ref, w_q_ref, w_o_ref, w_qk_ref, vw_ref):
    m = _rms(mem_ref[...], n_mem_ref[...]).astype(BF16)
    width = X_HEADS * X_HEAD_DIM
    k = _dot(m, w_kv_ref[:, :width].astype(BF16))
    v = _dot(m, w_kv_ref[:, width:].astype(BF16))
    for hd in range(X_HEADS):
        dims = slice(hd * X_HEAD_DIM, (hd + 1) * X_HEAD_DIM)
        mems = slice(hd * N_MEM, (hd + 1) * N_MEM)
        kt = k[:, dims].T.astype(BF16)
        w_qk = _dot(w_q_ref[:, dims].astype(BF16), kt) * (X_HEAD_DIM ** -0.5)
        w_qk_ref[:, mems] = w_qk.astype(BF16)
        vw = _dot(v[:, dims].astype(BF16), w_o_ref[dims, :].astype(BF16))
        vw_ref[mems, :] = vw.astype(BF16)


def _linear_scan(a, b, h0, a_scr, b_scr, h_scr):
    t, c = a.shape
    groups = t // SUBLANES
    a3 = a.reshape(groups, SUBLANES, c)
    b3 = b.reshape(groups, SUBLANES, c)
    row = lax.broadcasted_iota(jnp.int32, a3.shape, 1)
    shift = 1
    while shift < SUBLANES:
        keep = row >= shift
        a_prev = jnp.where(keep, pltpu.roll(a3, shift, 1), 1.0)
        b_prev = jnp.where(keep, pltpu.roll(b3, shift, 1), 0.0)
        b3 = a3 * b_prev + b3
        a3 = a3 * a_prev
        shift *= 2
    a_scr[...] = a3
    b_scr[...] = b3
    carry = h0
    for g in range(groups):
        blk = a_scr[g] * carry + b_scr[g]
        h_scr[g] = blk
        carry = blk[SUBLANES - 1:SUBLANES, :]
    return h_scr[...].reshape(t, c), carry


def _stage_a(x_ref, span, p, h_slot, hn_buf, h_carry, xr_scr, xr_next, a_scr, b_scr, h_scr):
    yield
    x = x_ref[span, :]
    t = x.shape[0]
    w_in_ref = p["w_in"]
    hn = _rms(x, p["n_pre_mix"][...]).astype(BF16)

    yield
    u = _gelu(_dot(hn, w_in_ref[:, 0:G_WIDTH]))
    yield
    gv = _gelu(_dot(hn, w_in_ref[:, G_WIDTH:2 * G_WIDTH]))
    mu = jnp.mean(gv, axis=-1, keepdims=True)
    cen = gv - mu
    var = jnp.mean(cen * cen, axis=-1, keepdims=True)
    v = (cen * lax.rsqrt(var + EPS) * p["ln_v_g"][...] + p["ln_v_b"][...]).astype(BF16)
    yield
    tri_r = lax.broadcasted_iota(jnp.int32, (CHUNK, CHUNK), 0)
    tri_c = lax.broadcasted_iota(jnp.int32, (CHUNK, CHUNK), 1)
    causal = tri_r >= tri_c
    ws = [jnp.where(causal, p["w_s"][hd], 0.0).astype(BF16) for hd in range(G_HEADS)]
    bs = p["b_s"][...]
    rows = []
    for c in range(t // CHUNK):
        cols = []
        for hd in range(G_HEADS):
            vb = v[c * CHUNK:(c + 1) * CHUNK, hd * G_HEAD_DIM:(hd + 1) * G_HEAD_DIM]
            cols.append(_dot(ws[hd], vb))
        rows.append(jnp.concatenate(cols, axis=1) + bs)
    s = jnp.concatenate(rows, axis=0)
    y_g = _rms(u * s, p["g_out_gmlp"][...]).astype(BF16)

    yield
    xr = _dot(hn, w_in_ref[:, 2 * G_WIDTH:2 * G_WIDTH + R_WIDTH])
    xr_scr[SUBLANES:SUBLANES + t, :] = xr
    cw = p["conv_w"][...]
    xc = p["conv_b"][...]
    for k in range(CONV_W):
        off = SUBLANES - (CONV_W - 1) + k
        xc = xc + xr_scr[off:off + t, :] * cw[k:k + 1, :]
    xr_next[0:SUBLANES, :] = xr[t - SUBLANES:t, :]
    xc_b = xc.astype(BF16)
    yield
    ga, gx = [], []
    for blk in range(R_WIDTH // GATE_BLOCK):
        g2 = _dot(xc_b[:, blk * GATE_BLOCK:(blk + 1) * GATE_BLOCK], p["w_gate_lru"][blk])
        ga.append(g2[:, :GATE_BLOCK])
        gx.append(g2[:, GATE_BLOCK:])
    r = _sigmoid(jnp.concatenate(ga, axis=1) + p["b_a"][...])
    i = _sigmoid(jnp.concatenate(gx, axis=1) + p["b_x"][...])
    nl = -p["lam"][...]
    softplus = jnp.maximum(nl, 0.0) + jnp.log1p(jnp.exp(-jnp.abs(nl)))
    neg_log_a = r * (RG_C * softplus)
    a = jnp.exp2(neg_log_a * NEG_LOG2E)
    mult = jnp.sqrt(jnp.tanh(neg_log_a) * (a * a + 1.0))
    bterm = mult * (i * xc)
    yield
    hseq, h_last = _linear_scan(a, bterm, h_carry[...], a_scr, b_scr, h_scr)
    h_carry[...] = h_last
    yield
    gr = _dot(hn, w_in_ref[:, 2 * G_WIDTH + R_WIDTH:])
    y_r = _rms(_gelu(gr) * hseq, p["g_out_lru"][...]).astype(BF16)
    y_cat = jnp.concatenate([y_g, y_r], axis=1)
    block_rows = t // OUT_ROW_BLOCKS
    blocks = [slice(r * block_rows, (r + 1) * block_rows) for r in range(OUT_ROW_BLOCKS)]
    h1s = []
    for rs in blocks:
        yield
        y = _dot(y_cat[rs, :], p["w_out"][...])
        h1s.append(x[rs, :] + _rms(y, p["n_post_mix"][...]))

    probs = []
    for h1 in h1s:
        yield
        hn = _rms(h1, p["n_pre_x"][...]).astype(BF16)
        sc_all = _dot(hn, p["w_qk"][...])
        pr = []
        for hd in range(X_HEADS):
            sc = sc_all[:, hd * N_MEM:(hd + 1) * N_MEM]
            m = jnp.max(sc, axis=-1, keepdims=True)
            e = jnp.exp(sc - m)
            pr.append((e / jnp.sum(e, axis=-1, keepdims=True)).astype(BF16))
        probs.append(jnp.concatenate(pr, axis=1))
    h2s, hn2s = [], []
    for h1, pr in zip(h1s, probs):
        yield
        y = _dot(pr, p["vw"][...])
        h2s.append(h1 + _rms(y, p["n_post_x"][...]))
        hn2s.append(_rms(h2s[-1], p["n_pre_ffn"][...]).astype(BF16))
    yield _AFTER_OTHERS
    h_slot[span, :] = jnp.concatenate(h2s, axis=0)
    hn_buf[span, :] = jnp.concatenate(hn2s, axis=0)


def _stage_a_halves(x_ref, p, h_slot, hn_buf, h_carry, xr_scr, a_scr, b_scr, h_scr):
    rows = x_ref.shape[0] // ROW_HALVES
    gens = [_stage_a(x_ref, slice(r * rows, (r + 1) * rows), p, h_slot, hn_buf, h_carry,
                     xr_scr.at[r], xr_scr.at[(r + 1) % ROW_HALVES], a_scr.at[r], b_scr.at[r],
                     h_scr.at[r]) for r in range(ROW_HALVES)]
    for gen in gens:
        next(gen)
    for _ in _STAGE_A_PHASES:
        for gen in gens:
            yield
            next(gen)
    yield _AFTER_OTHERS
    for gen in gens:
        next(gen, None)


_STAGE_A_PHASES = (
    (0, 800),
    (1024, 500),
    (1024, 1000),
    (500, 600),
    (1024, 700),
    (512, 1800),
    (0, 800),
    (1024, 1100),
) + (
    ((2048 // OUT_ROW_BLOCKS, 900 // OUT_ROW_BLOCKS),) * OUT_ROW_BLOCKS
    + ((2048 // OUT_ROW_BLOCKS, 2000 // OUT_ROW_BLOCKS),) * OUT_ROW_BLOCKS
    + ((2048 // OUT_ROW_BLOCKS, 1700 // OUT_ROW_BLOCKS),) * OUT_ROW_BLOCKS
)
STAGE_A_COST = tuple((m // ROW_HALVES, v // ROW_HALVES)
                     for m, v in _STAGE_A_PHASES for _ in range(ROW_HALVES))


def _stage_b(h_buf, hn_buf, p, act_scr, o_ref):
    for c in range(D_FF // FF_CHUNK):
        yield
        sl = slice(c * FF_CHUNK, (c + 1) * FF_CHUNK)
        g = _dot(hn_buf[...], p["w_gate"][:, sl])
        up = _dot(hn_buf[...], p["w_up"][:, sl])
        act_scr[:, sl] = (g * _sigmoid(g) * up).astype(BF16)
    rows = act_scr.shape[0] // DOWN_ROW_BLOCKS
    cols = D_MODEL // DOWN_COL_BLOCKS
    for r in range(DOWN_ROW_BLOCKS):
        rs = slice(r * rows, (r + 1) * rows)
        ys = []
        for c in range(DOWN_COL_BLOCKS):
            yield
            ys.append(_dot(act_scr[rs, :], p["w_down"][:, c * cols:(c + 1) * cols]))
        y = jnp.concatenate(ys, axis=1)
        o_ref[rs, :] = h_buf[rs, :] + _rms(y, p["n_post_ffn"][...])


_DOWN_PIECE = (5632 // (DOWN_ROW_BLOCKS * DOWN_COL_BLOCKS), 900 // (DOWN_ROW_BLOCKS * DOWN_COL_BLOCKS))
STAGE_B_COST = (((1024, 250),) * (D_FF // FF_CHUNK)
                + (_DOWN_PIECE,) * (DOWN_ROW_BLOCKS * DOWN_COL_BLOCKS))

_AFTER_OTHERS = object()


def _interleave(lead, lead_cost, fill, fill_cost):
    fill_total = sum(m for m, _ in fill_cost)
    valu_total = sum(v for _, v in lead_cost)
    next(lead)
    next(fill)
    fill_i, fill_done, valu_done, marker = 0, 0.0, 0.0, None
    for k in range(len(lead_cost)):
        marker = next(lead)
        valu_done += lead_cost[k][1]
        target = fill_total * valu_done / valu_total
        while fill_i < len(fill_cost) and fill_done + fill_cost[fill_i][0] / 2 <= target:
            next(fill, None)
            fill_done += fill_cost[fill_i][0]
            fill_i += 1
    if marker is not _AFTER_OTHERS:
        raise ValueError("lead_cost does not match the lead generator's phases")
    for _ in range(fill_i, len(fill_cost)):
        next(fill, None)
    leftovers = [next(fill, None), next(lead, None)]
    if leftovers != [None, None]:
        raise ValueError("phase generators yielded more phases than their cost tables list")


_PARAM_ORDER = (
    "n_pre_mix", "w_in", "ln_v_g", "ln_v_b", "w_s", "b_s", "conv_w", "conv_b", "w_gate_lru",
    "b_a", "b_x", "lam", "g_out_gmlp", "g_out_lru", "w_out", "n_post_mix",
    "n_pre_x", "w_qk", "vw", "n_post_x",
    "n_pre_ffn", "w_gate", "w_up", "w_down", "n_post_ffn",
)
_MATMUL_WEIGHTS = ("w_in", "w_out", "w_gate", "w_up", "w_down")
LOAD_ROWS = 128
LOAD_SLOTS = 4


def _load_weights(hbm_refs, vmem_refs, stage, sems):
    jobs = [(src, dst, r0) for src, dst in zip(hbm_refs, vmem_refs)
            for r0 in range(0, src.shape[0], LOAD_ROWS)]

    def copy(n):
        src, _, r0 = jobs[n]
        slot = n % LOAD_SLOTS
        return pltpu.make_async_copy(src.at[pl.ds(r0, LOAD_ROWS), :],
                                     stage.at[slot, :, pl.ds(0, src.shape[1])], sems.at[slot])

    ahead = LOAD_SLOTS - 1
    for n in range(min(ahead, len(jobs))):
        copy(n).start()
    for n, (src, dst, r0) in enumerate(jobs):
        if n + ahead < len(jobs):
            copy(n + ahead).start()
        copy(n).wait()
        dst[pl.ds(r0, LOAD_ROWS), :] = stage[n % LOAD_SLOTS, :, 0:src.shape[1]].astype(BF16)


def _layer_kernel(tiles_per_seq, x_ref, *rest):
    n_p, n_w = len(_PARAM_ORDER), len(_MATMUL_WEIGHTS)
    p = dict(zip(_PARAM_ORDER, rest[:n_p]))
    o_ref = rest[n_p]
    scratch = rest[n_p + 1:]
    weights, (w_stage, w_sems) = scratch[:n_w], scratch[n_w:n_w + 2]
    h_buf, hn_buf, act_scr, h_carry, xr_scr, a_scr, b_scr, h_scr = scratch[n_w + 2:]
    step = pl.program_id(0)
    n_tiles = pl.num_programs(0) - (PIPELINE_DEPTH - 1)
    tile = jnp.minimum(step, n_tiles - 1)

    @pl.when(step == 0)
    def _():
        _load_weights([p[k] for k in _MATMUL_WEIGHTS], weights, w_stage, w_sems)
        h_buf[...] = jnp.zeros_like(h_buf)
        hn_buf[...] = jnp.zeros_like(hn_buf)

    p.update(zip(_MATMUL_WEIGHTS, weights))

    @pl.when(lax.rem(tile, tiles_per_seq) == 0)
    def _():
        xr_scr[0, 0:SUBLANES, :] = jnp.zeros((SUBLANES, R_WIDTH), F32)
        h_carry[...] = jnp.zeros_like(h_carry)

    stage_a = _stage_a_halves(x_ref, p, h_buf, hn_buf, h_carry, xr_scr, a_scr, b_scr, h_scr)
    stage_b = _stage_b(h_buf, hn_buf, p, act_scr, o_ref)
    _interleave(stage_a, STAGE_A_COST, stage_b, STAGE_B_COST)


def _resident(shape):
    zeros = (0,) * len(shape)
    return pl.BlockSpec(shape, lambda *_: zeros, pipeline_mode=pl.Buffered(1))


def _block_diag(w):
    h, d, _ = w.shape
    eye = jnp.eye(h, dtype=w.dtype)
    return jnp.einsum('hij,hk->hikj', w, eye).reshape(h * d, h * d)


def _row(p):
    return p.reshape(1, -1).astype(F32)


def kernel(x, mem, w_in, ln_v_g, ln_v_b, w_s, b_s, conv_w, conv_b, w_a, b_a, w_x, b_x, lam,
           g_out_gmlp, g_out_lru, w_out, w_q, w_kv, w_o, w_gate, w_up, w_down, n_pre_mix,
           n_post_mix, n_pre_x, n_mem, n_post_x, n_pre_ffn, n_post_ffn):
    batch, seq, _ = x.shape
    depth = w_in.shape[0]
    t = TOKENS
    assert seq % t == 0 and t % CHUNK == 0
    tiles_per_seq = seq // t
    n_tiles = batch * tiles_per_seq
    heads_per_block = GATE_BLOCK // R_HEAD_DIM

    def tile_of_a(i):
        return jnp.minimum(i, n_tiles - 1)

    def tile_of_b(i):
        return jnp.maximum(i - (PIPELINE_DEPTH - 1), 0)

    x_spec = pl.BlockSpec((None, t, D_MODEL),
                          lambda i: (tile_of_a(i) // tiles_per_seq, tile_of_a(i) % tiles_per_seq, 0))
    o_spec = pl.BlockSpec((None, t, D_MODEL),
                          lambda i: (tile_of_b(i) // tiles_per_seq, tile_of_b(i) % tiles_per_seq, 0))
    att_width = X_HEADS * N_MEM
    w_qk_spec = pl.BlockSpec((None, D_MODEL, att_width), lambda i: (tile_of_a(i) // tiles_per_seq, 0, 0),
                             pipeline_mode=pl.Buffered(1))
    vw_spec = pl.BlockSpec((None, att_width, D_MODEL), lambda i: (tile_of_a(i) // tiles_per_seq, 0, 0),
                           pipeline_mode=pl.Buffered(1))

    h = x
    for l in range(depth):
        kv_in = [mem, _row(n_mem[l]), w_kv[l], w_q[l], w_o[l]]
        w_qk, vw = pl.pallas_call(
            _kv_kernel,
            grid=(batch,),
            in_specs=[pl.BlockSpec((None, N_MEM, D_MODEL), lambda b: (b, 0, 0))]
                     + [_resident(a.shape) for a in kv_in[1:]],
            out_specs=[pl.BlockSpec((None, D_MODEL, att_width), lambda b: (b, 0, 0)),
                       pl.BlockSpec((None, att_width, D_MODEL), lambda b: (b, 0, 0))],
            out_shape=[jax.ShapeDtypeStruct((batch, D_MODEL, att_width), BF16),
                       jax.ShapeDtypeStruct((batch, att_width, D_MODEL), BF16)],
            compiler_params=pltpu.CompilerParams(dimension_semantics=("arbitrary",),
                                                 vmem_limit_bytes=VMEM_LIMIT_BYTES),
            name="kv_proj",
        )(*kv_in)

        gate_blocks = []
        for blk in range(R_WIDTH // GATE_BLOCK):
            hs = slice(blk * heads_per_block, (blk + 1) * heads_per_block)
            gate_blocks.append(jnp.concatenate(
                [_block_diag(w_a[l, hs]), _block_diag(w_x[l, hs])], axis=1))

        params = {
            "n_pre_mix": _row(n_pre_mix[l]),
            "w_in": w_in[l],
            "ln_v_g": _row(ln_v_g[l]),
            "ln_v_b": _row(ln_v_b[l]),
            "w_s": w_s[l],
            "b_s": jnp.repeat(b_s[l].T, G_HEAD_DIM, axis=1).astype(F32),
            "conv_w": conv_w[l].astype(F32),
            "conv_b": _row(conv_b[l]),
            "w_gate_lru": jnp.stack(gate_blocks).astype(BF16),
            "b_a": _row(b_a[l]),
            "b_x": _row(b_x[l]),
            "lam": _row(lam[l]),
            "g_out_gmlp": _row(g_out_gmlp[l]),
            "g_out_lru": _row(g_out_lru[l]),
            "w_out": w_out[l],
            "n_post_mix": _row(n_post_mix[l]),
            "n_pre_x": _row(n_pre_x[l]),
            "w_qk": w_qk,
            "vw": vw,
            "n_post_x": _row(n_post_x[l]),
            "n_pre_ffn": _row(n_pre_ffn[l]),
            "w_gate": w_gate[l],
            "w_up": w_up[l],
            "w_down": w_down[l],
            "n_post_ffn": _row(n_post_ffn[l]),
        }
        specs = {"w_qk": w_qk_spec, "vw": vw_spec}
        specs.update({k: pl.BlockSpec(memory_space=pl.ANY) for k in _MATMUL_WEIGHTS})
        in_specs = [x_spec] + [specs.get(k) or _resident(params[k].shape) for k in _PARAM_ORDER]
        stage_cols = max(params[k].shape[1] for k in _MATMUL_WEIGHTS)
        assert all(params[k].shape[0] % LOAD_ROWS == 0 for k in _MATMUL_WEIGHTS)

        def layer_body(*refs):
            _layer_kernel(tiles_per_seq, *refs)

        h = pl.pallas_call(
            layer_body,
            grid=(n_tiles + PIPELINE_DEPTH - 1,),
            in_specs=in_specs,
            out_specs=o_spec,
            out_shape=jax.ShapeDtypeStruct((batch, seq, D_MODEL), F32),
            scratch_shapes=[pltpu.VMEM(params[k].shape, BF16) for k in _MATMUL_WEIGHTS] + [
                pltpu.VMEM((LOAD_SLOTS, LOAD_ROWS, stage_cols), F32),
                pltpu.SemaphoreType.DMA((LOAD_SLOTS,)),
                pltpu.VMEM((t, D_MODEL), F32),
                pltpu.VMEM((t, D_MODEL), BF16),
                pltpu.VMEM((t, D_FF), BF16),
                pltpu.VMEM((1, R_WIDTH), F32),
                pltpu.VMEM((ROW_HALVES, t // ROW_HALVES + SUBLANES, R_WIDTH), F32),
            ] + [pltpu.VMEM((ROW_HALVES, t // ROW_HALVES // SUBLANES, SUBLANES, R_WIDTH), F32)] * 3,
            compiler_params=pltpu.CompilerParams(dimension_semantics=("arbitrary",),
                                                 vmem_limit_bytes=VMEM_LIMIT_BYTES),
            name="layer",
        )(h, *[params[k] for k in _PARAM_ORDER])
    return h
```

```python
import math

import jax
import jax.numpy as jnp
from jax import lax
from jax.experimental import pallas as pl
from jax.experimental.pallas import tpu as pltpu

D_MODEL = 1024
G_WIDTH = 512
R_WIDTH = 512
G_HEADS = 4
G_HEAD_DIM = 128
CHUNK = 128
R_HEADS = 8
R_HEAD_DIM = 64
CONV_W = 4
RG_C = 8.0
N_MEM = 256
X_HEADS = 4
X_HEAD_DIM = 256
D_FF = 2816
EPS = 1e-6

SUBLANES = 8
NEG_LOG2E = -math.log2(math.e)
GATE_BLOCK = 256
TOKENS = 512
FF_CHUNK = 256
DOWN_ROW_BLOCKS = 2
DOWN_COL_BLOCKS = 2
ROW_HALVES = 1
OUT_ROW_BLOCKS = 2
PIPELINE_DEPTH = 2
VMEM_LIMIT_BYTES = 60 * 1024 * 1024

BF16 = jnp.bfloat16
F32 = jnp.float32


def _rms(x, g):
    return x * lax.rsqrt(jnp.mean(x * x, axis=-1, keepdims=True) + EPS) * g


def _sigmoid(x):
    return 1.0 / (1.0 + jnp.exp2(x * NEG_LOG2E))


def _gelu(x):
    k1 = 2.0 * math.sqrt(2.0 / math.pi) * NEG_LOG2E
    return x / (1.0 + jnp.exp2(x * (k1 + (k1 * 0.044715) * (x * x))))


def _dot(a, b):
    return jnp.dot(a, b, preferred_element_type=F32)


def _kv_kernel(mem_ref, n_mem_ref, w_kv_ref, w_q_ref, w_o_ref, w_qk_ref, vw_ref):
    m = _rms(mem_ref[...], n_mem_ref[...]).astype(BF16)
    width = X_HEADS * X_HEAD_DIM
    k = _dot(m, w_kv_ref[:, :width].astype(BF16))
    v = _dot(m, w_kv_ref[:, width:].astype(BF16))
    for hd in range(X_HEADS):
        dims = slice(hd * X_HEAD_DIM, (hd + 1) * X_HEAD_DIM)
        mems = slice(hd * N_MEM, (hd + 1) * N_MEM)
        kt = k[:, dims].T.astype(BF16)
        w_qk = _dot(w_q_ref[:, dims].astype(BF16), kt) * (X_HEAD_DIM ** -0.5)
        w_qk_ref[:, mems] = w_qk.astype(BF16)
        vw = _dot(v[:, dims].astype(BF16), w_o_ref[dims, :].astype(BF16))
        vw_ref[mems, :] = vw.astype(BF16)


def _linear_scan(a, b, h0, a_scr, b_scr, h_scr):
    t, c = a.shape
    groups = t // SUBLANES
    a3 = a.reshape(groups, SUBLANES, c)
    b3 = b.reshape(groups, SUBLANES, c)
    row = lax.broadcasted_iota(jnp.int32, a3.shape, 1)
    shift = 1
    while shift < SUBLANES:
        keep = row >= shift
        a_prev = jnp.where(keep, pltpu.roll(a3, shift, 1), 1.0)
        b_prev = jnp.where(keep, pltpu.roll(b3, shift, 1), 0.0)
        b3 = a3 * b_prev + b3
        a3 = a3 * a_prev
        shift *= 2
    a_scr[...] = a3
    b_scr[...] = b3
    carry = h0
    for g in range(groups):
        blk = a_scr[g] * carry + b_scr[g]
        h_scr[g] = blk
        carry = blk[SUBLANES - 1:SUBLANES, :]
    return h_scr[...].reshape(t, c), carry


def _stage_a(x_ref, span, p, h_slot, hn_buf, h_carry, xr_scr, xr_next, a_scr, b_scr, h_scr):
    yield
    x = x_ref[span, :]
    t = x.shape[0]
    w_in_ref = p["w_in"]
    hn = _rms(x, p["n_pre_mix"][...]).astype(BF16)

    yield
    u = _gelu(_dot(hn, w_in_ref[:, 0:G_WIDTH]))
    yield
    gv = _gelu(_dot(hn, w_in_ref[:, G_WIDTH:2 * G_WIDTH]))
    mu = jnp.mean(gv, axis=-1, keepdims=True)
    cen = gv - mu
    var = jnp.mean(cen * cen, axis=-1, keepdims=True)
    v = (cen * lax.rsqrt(var + EPS) * p["ln_v_g"][...] + p["ln_v_b"][...]).astype(BF16)
    yield
    tri_r = lax.broadcasted_iota(jnp.int32, (CHUNK, CHUNK), 0)
    tri_c = lax.broadcasted_iota(jnp.int32, (CHUNK, CHUNK), 1)
    causal = tri_r >= tri_c
    ws = [jnp.where(causal, p["w_s"][hd], 0.0).astype(BF16) for hd in range(G_HEADS)]
    bs = p["b_s"][...]
    rows = []
    for c in range(t // CHUNK):
        cols = []
        for hd in range(G_HEADS):
            vb = v[c * CHUNK:(c + 1) * CHUNK, hd * G_HEAD_DIM:(hd + 1) * G_HEAD_DIM]
            cols.append(_dot(ws[hd], vb))
        rows.append(jnp.concatenate(cols, axis=1) + bs)
    s = jnp.concatenate(rows, axis=0)
    y_g = _rms(u * s, p["g_out_gmlp"][...]).astype(BF16)

    yield
    xr = _dot(hn, w_in_ref[:, 2 * G_WIDTH:2 * G_WIDTH + R_WIDTH])
    xr_scr[SUBLANES:SUBLANES + t, :] = xr
    cw = p["conv_w"][...]
    xc = p["conv_b"][...]
    for k in range(CONV_W):
        off = SUBLANES - (CONV_W - 1) + k
        xc = xc + xr_scr[off:off + t, :] * cw[k:k + 1, :]
    xr_next[0:SUBLANES, :] = xr[t - SUBLANES:t, :]
    xc_b = xc.astype(BF16)
    yield
    ga, gx = [], []
    for blk in range(R_WIDTH // GATE_BLOCK):
        g2 = _dot(xc_b[:, blk * GATE_BLOCK:(blk + 1) * GATE_BLOCK], p["w_gate_lru"][blk])
        ga.append(g2[:, :GATE_BLOCK])
        gx.append(g2[:, GATE_BLOCK:])
    r = _sigmoid(jnp.concatenate(ga, axis=1) + p["b_a"][...])
    i = _sigmoid(jnp.concatenate(gx, axis=1) + p["b_x"][...])
    nl = -p["lam"][...]
    softplus = jnp.maximum(nl, 0.0) + jnp.log1p(jnp.exp(-jnp.abs(nl)))
    neg_log_a = r * (RG_C * softplus)
    a = jnp.exp2(neg_log_a * NEG_LOG2E)
    mult = jnp.sqrt(jnp.tanh(neg_log_a) * (a * a + 1.0))
    bterm = mult * (i * xc)
    yield
    hseq, h_last = _linear_scan(a, bterm, h_carry[...], a_scr, b_scr, h_scr)
    h_carry[...] = h_last
    block_rows = t // OUT_ROW_BLOCKS
    blocks = [slice(r * block_rows, (r + 1) * block_rows) for r in range(OUT_ROW_BLOCKS)]
    y_cats = []
    for rs in blocks:
        yield
        gr = _dot(hn[rs, :], w_in_ref[:, 2 * G_WIDTH + R_WIDTH:])
        y_r = _rms(_gelu(gr) * hseq[rs, :], p["g_out_lru"][...]).astype(BF16)
        y_cats.append(jnp.concatenate([y_g[rs, :], y_r], axis=1))
    h1s = []
    for rs, y_cat in zip(blocks, y_cats):
        yield
        y = _dot(y_cat, p["w_out"][...])
        h1s.append(x[rs, :] + _rms(y, p["n_post_mix"][...]))

    probs = []
    for h1 in h1s:
        yield
        hn = _rms(h1, p["n_pre_x"][...]).astype(BF16)
        sc_all = _dot(hn, p["w_qk"][...])
        pr = []
        for hd in range(X_HEADS):
            sc = sc_all[:, hd * N_MEM:(hd + 1) * N_MEM]
            m = jnp.max(sc, axis=-1, keepdims=True)
            e = jnp.exp(sc - m)
            pr.append((e / jnp.sum(e, axis=-1, keepdims=True)).astype(BF16))
        probs.append(jnp.concatenate(pr, axis=1))
    h2s, hn2s = [], []
    for h1, pr in zip(h1s, probs):
        yield
        y = _dot(pr, p["vw"][...])
        h2s.append(h1 + _rms(y, p["n_post_x"][...]))
        hn2s.append(_rms(h2s[-1], p["n_pre_ffn"][...]).astype(BF16))
    yield _AFTER_OTHERS
    h_slot[span, :] = jnp.concatenate(h2s, axis=0)
    hn_buf[span, :] = jnp.concatenate(hn2s, axis=0)


def _stage_a_halves(x_ref, p, h_slot, hn_buf, h_carry, xr_scr, a_scr, b_scr, h_scr):
    rows = x_ref.shape[0] // ROW_HALVES
    gens = [_stage_a(x_ref, slice(r * rows, (r + 1) * rows), p, h_slot, hn_buf, h_carry,
                     xr_scr.at[r], xr_scr.at[(r + 1) % ROW_HALVES], a_scr.at[r], b_scr.at[r],
                     h_scr.at[r]) for r in range(ROW_HALVES)]
    for gen in gens:
        next(gen)
    for _ in _STAGE_A_PHASES:
        for gen in gens:
            yield
            next(gen)
    yield _AFTER_OTHERS
    for gen in gens:
        next(gen, None)


_STAGE_A_PHASES = (
    (0, 800),
    (1024, 500),
    (1024, 1000),
    (500, 600),
    (1024, 700),
    (512, 1800),
    (0, 800),
) + (
    ((1024 // OUT_ROW_BLOCKS, 1100 // OUT_ROW_BLOCKS),) * OUT_ROW_BLOCKS
    + ((2048 // OUT_ROW_BLOCKS, 900 // OUT_ROW_BLOCKS),) * OUT_ROW_BLOCKS
    + ((2048 // OUT_ROW_BLOCKS, 2000 // OUT_ROW_BLOCKS),) * OUT_ROW_BLOCKS
    + ((2048 // OUT_ROW_BLOCKS, 1700 // OUT_ROW_BLOCKS),) * OUT_ROW_BLOCKS
)
STAGE_A_COST = tuple((m // ROW_HALVES, v // ROW_HALVES)
                     for m, v in _STAGE_A_PHASES for _ in range(ROW_HALVES))


def _stage_b(h_buf, hn_buf, p, act_scr, o_ref):
    for c in range(D_FF // FF_CHUNK):
        yield
        sl = slice(c * FF_CHUNK, (c + 1) * FF_CHUNK)
        g = _dot(hn_buf[...], p["w_gate"][:, sl])
        up = _dot(hn_buf[...], p["w_up"][:, sl])
        act_scr[:, sl] = (g * _sigmoid(g) * up).astype(BF16)
    rows = act_scr.shape[0] // DOWN_ROW_BLOCKS
    cols = D_MODEL // DOWN_COL_BLOCKS
    for r in range(DOWN_ROW_BLOCKS):
        rs = slice(r * rows, (r + 1) * rows)
        ys = []
        for c in range(DOWN_COL_BLOCKS):
            yield
            ys.append(_dot(act_scr[rs, :], p["w_down"][:, c * cols:(c + 1) * cols]))
        y = jnp.concatenate(ys, axis=1)
        o_ref[rs, :] = h_buf[rs, :] + _rms(y, p["n_post_ffn"][...])


_DOWN_PIECE = (5632 // (DOWN_ROW_BLOCKS * DOWN_COL_BLOCKS), 900 // (DOWN_ROW_BLOCKS * DOWN_COL_BLOCKS))
STAGE_B_COST = (((1024, 250),) * (D_FF // FF_CHUNK)
                + (_DOWN_PIECE,) * (DOWN_ROW_BLOCKS * DOWN_COL_BLOCKS))

_AFTER_OTHERS = object()


def _interleave(lead, lead_cost, fill, fill_cost):
    fill_total = sum(m for m, _ in fill_cost)
    valu_total = sum(v for _, v in lead_cost)
    next(lead)
    next(fill)
    fill_i, fill_done, valu_done, marker = 0, 0.0, 0.0, None
    for k in range(len(lead_cost)):
        marker = next(lead)
        valu_done += lead_cost[k][1]
        target = fill_total * valu_done / valu_total
        while fill_i < len(fill_cost) and fill_done + fill_cost[fill_i][0] / 2 <= target:
            next(fill, None)
            fill_done += fill_cost[fill_i][0]
            fill_i += 1
    if marker is not _AFTER_OTHERS:
        raise ValueError("lead_cost does not match the lead generator's phases")
    for _ in range(fill_i, len(fill_cost)):
        next(fill, None)
    leftovers = [next(fill, None), next(lead, None)]
    if leftovers != [None, None]:
        raise ValueError("phase generators yielded more phases than their cost tables list")


_PARAM_ORDER = (
    "n_pre_mix", "w_in", "ln_v_g", "ln_v_b", "w_s", "b_s", "conv_w", "conv_b", "w_gate_lru",
    "b_a", "b_x", "lam", "g_out_gmlp", "g_out_lru", "w_out", "n_post_mix",
    "n_pre_x", "w_qk", "vw", "n_post_x",
    "n_pre_ffn", "w_gate", "w_up", "w_down", "n_post_ffn",
)
_MATMUL_WEIGHTS = ("w_in", "w_out", "w_gate", "w_up", "w_down")
LOAD_ROWS = 128
LOAD_SLOTS = 4


def _load_weights(hbm_refs, vmem_refs, stage, sems):
    jobs = [(src, dst, r0) for src, dst in zip(hbm_refs, vmem_refs)
            for r0 in range(0, src.shape[0], LOAD_ROWS)]

    def copy(n):
        src, _, r0 = jobs[n]
        slot = n % LOAD_SLOTS
        return pltpu.make_async_copy(src.at[pl.ds(r0, LOAD_ROWS), :],
                                     stage.at[slot, :, pl.ds(0, src.shape[1])], sems.at[slot])

    ahead = LOAD_SLOTS - 1
    for n in range(min(ahead, len(jobs))):
        copy(n).start()
    for n, (src, dst, r0) in enumerate(jobs):
        if n + ahead < len(jobs):
            copy(n + ahead).start()
        copy(n).wait()
        dst[pl.ds(r0, LOAD_ROWS), :] = stage[n % LOAD_SLOTS, :, 0:src.shape[1]].astype(BF16)


def _layer_kernel(tiles_per_seq, x_ref, *rest):
    n_p, n_w = len(_PARAM_ORDER), len(_MATMUL_WEIGHTS)
    p = dict(zip(_PARAM_ORDER, rest[:n_p]))
    o_ref = rest[n_p]
    scratch = rest[n_p + 1:]
    weights, (w_stage, w_sems) = scratch[:n_w], scratch[n_w:n_w + 2]
    h_buf, hn_buf, act_scr, h_carry, xr_scr, a_scr, b_scr, h_scr = scratch[n_w + 2:]
    step = pl.program_id(0)
    n_tiles = pl.num_programs(0) - (PIPELINE_DEPTH - 1)
    tile = jnp.minimum(step, n_tiles - 1)

    @pl.when(step == 0)
    def _():
        _load_weights([p[k] for k in _MATMUL_WEIGHTS], weights, w_stage, w_sems)
        h_buf[...] = jnp.zeros_like(h_buf)
        hn_buf[...] = jnp.zeros_like(hn_buf)

    p.update(zip(_MATMUL_WEIGHTS, weights))

    @pl.when(lax.rem(tile, tiles_per_seq) == 0)
    def _():
        xr_scr[0, 0:SUBLANES, :] = jnp.zeros((SUBLANES, R_WIDTH), F32)
        h_carry[...] = jnp.zeros_like(h_carry)

    stage_a = _stage_a_halves(x_ref, p, h_buf, hn_buf, h_carry, xr_scr, a_scr, b_scr, h_scr)
    stage_b = _stage_b(h_buf, hn_buf, p, act_scr, o_ref)
    _interleave(stage_a, STAGE_A_COST, stage_b, STAGE_B_COST)


def _resident(shape):
    zeros = (0,) * len(shape)
    return pl.BlockSpec(shape, lambda *_: zeros, pipeline_mode=pl.Buffered(1))


def _block_diag(w):
    h, d, _ = w.shape
    eye = jnp.eye(h, dtype=w.dtype)
    return jnp.einsum('hij,hk->hikj', w, eye).reshape(h * d, h * d)


def _row(p):
    return p.reshape(1, -1).astype(F32)


def kernel(x, mem, w_in, ln_v_g, ln_v_b, w_s, b_s, conv_w, conv_b, w_a, b_a, w_x, b_x, lam,
           g_out_gmlp, g_out_lru, w_out, w_q, w_kv, w_o, w_gate, w_up, w_down, n_pre_mix,
           n_post_mix, n_pre_x, n_mem, n_post_x, n_pre_ffn, n_post_ffn):
    batch, seq, _ = x.shape
    depth = w_in.shape[0]
    t = TOKENS
    assert seq % t == 0 and t % CHUNK == 0
    tiles_per_seq = seq // t
    n_tiles = batch * tiles_per_seq
    heads_per_block = GATE_BLOCK // R_HEAD_DIM

    def tile_of_a(i):
        return jnp.minimum(i, n_tiles - 1)

    def tile_of_b(i):
        return jnp.maximum(i - (PIPELINE_DEPTH - 1), 0)

    x_spec = pl.BlockSpec((None, t, D_MODEL),
                          lambda i: (tile_of_a(i) // tiles_per_seq, tile_of_a(i) % tiles_per_seq, 0))
    o_spec = pl.BlockSpec((None, t, D_MODEL),
                          lambda i: (tile_of_b(i) // tiles_per_seq, tile_of_b(i) % tiles_per_seq, 0))
    att_width = X_HEADS * N_MEM
    w_qk_spec = pl.BlockSpec((None, D_MODEL, att_width), lambda i: (tile_of_a(i) // tiles_per_seq, 0, 0),
                             pipeline_mode=pl.Buffered(1))
    vw_spec = pl.BlockSpec((None, att_width, D_MODEL), lambda i: (tile_of_a(i) // tiles_per_seq, 0, 0),
                           pipeline_mode=pl.Buffered(1))

    h = x
    for l in range(depth):
        kv_in = [mem, _row(n_mem[l]), w_kv[l], w_q[l], w_o[l]]
        w_qk, vw = pl.pallas_call(
            _kv_kernel,
            grid=(batch,),
            in_specs=[pl.BlockSpec((None, N_MEM, D_MODEL), lambda b: (b, 0, 0))]
                     + [_resident(a.shape) for a in kv_in[1:]],
            out_specs=[pl.BlockSpec((None, D_MODEL, att_width), lambda b: (b, 0, 0)),
                       pl.BlockSpec((None, att_width, D_MODEL), lambda b: (b, 0, 0))],
            out_shape=[jax.ShapeDtypeStruct((batch, D_MODEL, att_width), BF16),
                       jax.ShapeDtypeStruct((batch, att_width, D_MODEL), BF16)],
            compiler_params=pltpu.CompilerParams(dimension_semantics=("arbitrary",),
                                                 vmem_limit_bytes=VMEM_LIMIT_BYTES),
            name="kv_proj",
        )(*kv_in)

        gate_blocks = []
        for blk in range(R_WIDTH // GATE_BLOCK):
            hs = slice(blk * heads_per_block, (blk + 1) * heads_per_block)
            gate_blocks.append(jnp.concatenate(
                [_block_diag(w_a[l, hs]), _block_diag(w_x[l, hs])], axis=1))

        params = {
            "n_pre_mix": _row(n_pre_mix[l]),
            "w_in": w_in[l],
            "ln_v_g": _row(ln_v_g[l]),
            "ln_v_b": _row(ln_v_b[l]),
            "w_s": w_s[l],
            "b_s": jnp.repeat(b_s[l].T, G_HEAD_DIM, axis=1).astype(F32),
            "conv_w": conv_w[l].astype(F32),
            "conv_b": _row(conv_b[l]),
            "w_gate_lru": jnp.stack(gate_blocks).astype(BF16),
            "b_a": _row(b_a[l]),
            "b_x": _row(b_x[l]),
            "lam": _row(lam[l]),
            "g_out_gmlp": _row(g_out_gmlp[l]),
            "g_out_lru": _row(g_out_lru[l]),
            "w_out": w_out[l],
            "n_post_mix": _row(n_post_mix[l]),
            "n_pre_x": _row(n_pre_x[l]),
            "w_qk": w_qk,
            "vw": vw,
            "n_post_x": _row(n_post_x[l]),
            "n_pre_ffn": _row(n_pre_ffn[l]),
            "w_gate": w_gate[l],
            "w_up": w_up[l],
            "w_down": w_down[l],
            "n_post_ffn": _row(n_post_ffn[l]),
        }
        specs = {"w_qk": w_qk_spec, "vw": vw_spec}
        specs.update({k: pl.BlockSpec(memory_space=pl.ANY) for k in _MATMUL_WEIGHTS})
        in_specs = [x_spec] + [specs.get(k) or _resident(params[k].shape) for k in _PARAM_ORDER]
        stage_cols = max(params[k].shape[1] for k in _MATMUL_WEIGHTS)
        assert all(params[k].shape[0] % LOAD_ROWS == 0 for k in _MATMUL_WEIGHTS)

        def layer_body(*refs):
            _layer_kernel(tiles_per_seq, *refs)

        h = pl.pallas_call(
            layer_body,
            grid=(n_tiles + PIPELINE_DEPTH - 1,),
            in_specs=in_specs,
            out_specs=o_spec,
            out_shape=jax.ShapeDtypeStruct((batch, seq, D_MODEL), F32),
            scratch_shapes=[pltpu.VMEM(params[k].shape, BF16) for k in _MATMUL_WEIGHTS] + [
                pltpu.VMEM((LOAD_SLOTS, LOAD_ROWS, stage_cols), F32),
                pltpu.SemaphoreType.DMA((LOAD_SLOTS,)),
                pltpu.VMEM((t, D_MODEL), F32),
                pltpu.VMEM((t, D_MODEL), BF16),
                pltpu.VMEM((t, D_FF), BF16),
                pltpu.VMEM((1, R_WIDTH), F32),
                pltpu.VMEM((ROW_HALVES, t // ROW_HALVES + SUBLANES, R_WIDTH), F32),
            ] + [pltpu.VMEM((ROW_HALVES, t // ROW_HALVES // SUBLANES, SUBLANES, R_WIDTH), F32)] * 3,
            compiler_params=pltpu.CompilerParams(dimension_semantics=("arbitrary",),
                                                 vmem_limit_bytes=VMEM_LIMIT_BYTES),
            name="layer",
        )(h, *[params[k] for k in _PARAM_ORDER])
    return h
```
